```python
import jax, jax.numpy as jnp
from jax import lax
import numpy as np

D_MODEL = 2048
BATCH = 16
SEQ = 2048
DEPTH = 4
DEC_BATCH = 16
DEC_SEQ = 64
PAST_LEN = 2048

CHUNK = 64
PAST_CHUNKS = 8
BAND_PAST = PAST_CHUNKS * CHUNK
BAND = BAND_PAST + CHUNK
D_CONV = D_MODEL // 2
CONV_WIDTH = 31
D_ATTN = D_MODEL // 2
HEAD_DIM = 64
N_HEADS = D_ATTN // HEAD_DIM
REL_CLIP = 128
N_REL = 2 * REL_CLIP + 1
D_FF = ((8 * D_MODEL // 3 + 255) // 256) * 256
D_IN = 2 * D_CONV + 3 * D_ATTN
EPS = 1e-6
NEG = -1e30

kernel_name = "streaming_conformer_conv_chunkattn_gated_hybrid"


def rms_norm(x, g):
    xf = x.astype(jnp.float32)
    y = xf * lax.rsqrt(jnp.mean(xf * xf, axis=-1, keepdims=True) + EPS)
    return (y * g.astype(jnp.float32)).astype(x.dtype)


def layer_norm(x, g, b):
    xf = x.astype(jnp.float32)
    mu = jnp.mean(xf, axis=-1, keepdims=True)
    var = jnp.mean(jnp.square(xf - mu), axis=-1, keepdims=True)
    y = (xf - mu) * lax.rsqrt(var + EPS)
    return (y * g.astype(jnp.float32) + b.astype(jnp.float32)).astype(x.dtype)


def mixer_inputs(x, g_norm, w_in):
    h = rms_norm(x, g_norm)
    B, T, _ = x.shape
    proj = h @ w_in
    a, b, q, k, v = jnp.split(proj, [D_CONV, 2 * D_CONV, 2 * D_CONV + D_ATTN, 2 * D_CONV + 2 * D_ATTN], axis=-1)
    u = a * jax.nn.sigmoid(b)
    heads = lambda t: t.reshape(B, T, N_HEADS, HEAD_DIM)
    return h, u, heads(q), heads(k), heads(v)


def conv_module(u, prefix, conv_w, conv_b, ln_g, ln_b, w_conv_out):
    u_ext = jnp.concatenate([prefix, u], axis=1)
    y = lax.conv_general_dilated(u_ext, conv_w[:, None, :], window_strides=(1,), padding='VALID',
                                 dimension_numbers=('NWC', 'WIO', 'NWC'),
                                 feature_group_count=D_CONV) + conv_b
    y = jax.nn.silu(layer_norm(y, ln_g, ln_b))
    return y @ w_conv_out, u_ext[:, -(CONV_WIDTH - 1):]


def chunk_attend(qc, kb, vb, q_pos, k_pos, rel_bias):
    s = jnp.einsum('bqhd,bkhd->bhqk', qc, kb).astype(jnp.float32) * (HEAD_DIM ** -0.5)
    rel = jnp.clip(q_pos[:, None] - k_pos[None, :], -REL_CLIP, REL_CLIP) + REL_CLIP
    s = s + rel_bias[:, rel].astype(jnp.float32)[None]
    qch = q_pos // CHUNK
    kch = k_pos // CHUNK
    mask = (k_pos[None, :] >= 0) & (kch[None, :] <= qch[:, None]) & (kch[None, :] >= qch[:, None] - PAST_CHUNKS)
    s = jnp.where(mask[None, None], s, NEG)
    p = jax.nn.softmax(s, axis=-1).astype(vb.dtype)
    return jnp.einsum('bhqk,bkhd->bqhd', p, vb)


def attn_prompt(q, k, v, rel_bias):
    B, T, H, hd = q.shape
    n_chunks = T // CHUNK
    pad = ((0, 0), (BAND_PAST, 0), (0, 0), (0, 0))
    k_pad = jnp.pad(k, pad)
    v_pad = jnp.pad(v, pad)

    def one_chunk(c):
        start = c * CHUNK
        qc = lax.dynamic_slice_in_dim(q, start, CHUNK, axis=1)
        kb = lax.dynamic_slice_in_dim(k_pad, start, BAND, axis=1)
        vb = lax.dynamic_slice_in_dim(v_pad, start, BAND, axis=1)
        q_pos = start + jnp.arange(CHUNK)
        k_pos = start - BAND_PAST + jnp.arange(BAND)
        return chunk_attend(qc, kb, vb, q_pos, k_pos, rel_bias)

    out = lax.map(one_chunk, jnp.arange(n_chunks))
    return jnp.moveaxis(out, 0, 1).reshape(B, T, H, hd)


def attn_sample(q, k, v, cache_k, cache_v, rel_bias):
    T = q.shape[1]
    keep = cache_k.shape[1]
    k_all = jnp.concatenate([cache_k, k], axis=1)
    v_all = jnp.concatenate([cache_v, v], axis=1)
    q_pos = PAST_LEN + jnp.arange(T)
    k_pos = PAST_LEN - keep + jnp.arange(keep + T)
    out = chunk_attend(q, k_all, v_all, q_pos, k_pos, rel_bias)
    return out, k_all[:, -keep:], v_all[:, -keep:]


def merge(h, y_conv, y_attn, w_o, w_gate, b_gate, w_out):
    B, T, _ = h.shape
    y_attn = y_attn.reshape(B, T, D_ATTN) @ w_o
    g = jax.nn.sigmoid(h @ w_gate + b_gate)
    g_conv, g_attn = jnp.split(g, 2, axis=-1)
    return (g_conv * y_conv + g_attn * y_attn) @ w_out


def ffn(x, g_norm, w_up, w_down):
    h = rms_norm(x, g_norm)
    gate, up = jnp.split(h @ w_up, 2, axis=-1)
    return (jax.nn.silu(gate) * up) @ w_down


def setup_inputs(seed: int = 0) -> dict:
    key = jax.random.key(seed)
    ks = jax.random.split(key, 24)
    keep = min(BAND_PAST, PAST_LEN)
    f = jnp.float32
    nrm = lambda k, shape, scale: jax.random.normal(k, shape, f) * scale
    gain = lambda k, shape: 1.0 + 0.05 * jax.random.normal(k, shape, f)
    return {
        "x_prompt": nrm(ks[0], (BATCH, SEQ, D_MODEL), 1.0),
        "x_sample": nrm(ks[1], (DEC_BATCH, DEC_SEQ, D_MODEL), 1.0),
        "cache_k": nrm(ks[2], (DEPTH, DEC_BATCH, keep, N_HEADS, HEAD_DIM), 1.0),
        "cache_v": nrm(ks[3], (DEPTH, DEC_BATCH, keep, N_HEADS, HEAD_DIM), 1.0),
        "state_conv": nrm(ks[4], (DEPTH, DEC_BATCH, CONV_WIDTH - 1, D_CONV), 0.5),
        "norm_mix": gain(ks[5], (DEPTH, D_MODEL)),
        "w_in": nrm(ks[6], (DEPTH, D_MODEL, D_IN), D_MODEL ** -0.5),
        "conv_w": nrm(ks[7], (DEPTH, CONV_WIDTH, D_CONV), CONV_WIDTH ** -0.5),
        "conv_b": nrm(ks[8], (DEPTH, D_CONV), 0.02),
        "ln_g": gain(ks[9], (DEPTH, D_CONV)),
        "ln_b": nrm(ks[10], (DEPTH, D_CONV), 0.02),
        "w_conv_out": nrm(ks[11], (DEPTH, D_CONV, D_MODEL), D_CONV ** -0.5),
        "rel_bias": nrm(ks[12], (DEPTH, N_HEADS, N_REL), 0.5),
        "w_o": nrm(ks[13], (DEPTH, D_ATTN, D_MODEL), D_ATTN ** -0.5),
        "w_gate": nrm(ks[14], (DEPTH, D_MODEL, 2 * D_MODEL), D_MODEL ** -0.5),
        "b_gate": nrm(ks[15], (DEPTH, 2 * D_MODEL), 0.1),
        "w_out": nrm(ks[16], (DEPTH, D_MODEL, D_MODEL), D_MODEL ** -0.5),
        "norm_ffn": gain(ks[17], (DEPTH, D_MODEL)),
        "w_up": nrm(ks[18], (DEPTH, D_MODEL, 2 * D_FF), D_MODEL ** -0.5),
        "w_down": nrm(ks[19], (DEPTH, D_FF, D_MODEL), D_FF ** -0.5),
        "norm_final": gain(ks[20], (D_MODEL,)),
    }


def reference(x_prompt, x_sample, cache_k, cache_v, state_conv, norm_mix, w_in, conv_w, conv_b,
              ln_g, ln_b, w_conv_out, rel_bias, w_o, w_gate, b_gate, w_out, norm_ffn, w_up,
              w_down, norm_final):
    xp, xs = x_prompt, x_sample
    B, T = xp.shape[0], xp.shape[1]
    keep_p = min(BAND_PAST, T)
    kp, vp, cp, ksl, vsl, csl = [], [], [], [], [], []
    for l in range(DEPTH):
        h, u, q, k, v = mixer_inputs(xp, norm_mix[l], w_in[l])
        zero_prefix = jnp.zeros((B, CONV_WIDTH - 1, D_CONV), u.dtype)
        y_conv, conv_state = conv_module(u, zero_prefix, conv_w[l], conv_b[l], ln_g[l], ln_b[l], w_conv_out[l])
        y_attn = attn_prompt(q, k, v, rel_bias[l])
        xp = xp + merge(h, y_conv, y_attn, w_o[l], w_gate[l], b_gate[l], w_out[l])
        xp = xp + ffn(xp, norm_ffn[l], w_up[l], w_down[l])
        kp.append(k[:, -keep_p:])
        vp.append(v[:, -keep_p:])
        cp.append(conv_state)
        h, u, q, k, v = mixer_inputs(xs, norm_mix[l], w_in[l])
        y_conv, conv_state = conv_module(u, state_conv[l], conv_w[l], conv_b[l], ln_g[l], ln_b[l], w_conv_out[l])
        y_attn, k_buf, v_buf = attn_sample(q, k, v, cache_k[l], cache_v[l], rel_bias[l])
        xs = xs + merge(h, y_conv, y_attn, w_o[l], w_gate[l], b_gate[l], w_out[l])
        xs = xs + ffn(xs, norm_ffn[l], w_up[l], w_down[l])
        ksl.append(k_buf)
        vsl.append(v_buf)
        csl.append(conv_state)
    y_prompt = rms_norm(xp, norm_final)
    y_sample = rms_norm(xs, norm_final)
    return (y_prompt, y_sample, jnp.stack(kp), jnp.stack(vp), jnp.stack(cp),
            jnp.stack(ksl), jnp.stack(vsl), jnp.stack(csl))
```

```python
import functools

import jax
import jax.numpy as jnp
from jax import lax
from jax.experimental import pallas as pl
from jax.experimental.pallas import tpu as pltpu

F32 = jnp.float32
BF16 = jnp.bfloat16

CHUNK = 64
PAST_CHUNKS = 8
BAND_PAST = PAST_CHUNKS * CHUNK
BAND = BAND_PAST + CHUNK
BAND_PAD = BAND + CHUNK
HEAD_DIM = 64
CONV_WIDTH = 31
CONV_HALO = 32
REL_CLIP = 128
EPS = 1e-6
NEG = -1e30

LANES = 128
PAIR = 2 * HEAD_DIM
VMEM_LIMIT_BYTES = 56 * 1024 * 1024


def _cparams(ndims):
    return pltpu.CompilerParams(dimension_semantics=("arbitrary",) * ndims,
                                vmem_limit_bytes=VMEM_LIMIT_BYTES)


def _sigmoid(x):
    return 1.0 / (1.0 + jnp.exp(-x))


def _rms_rows(x, g):
    ms = jnp.mean(x * x, axis=-1, keepdims=True)
    return (x * lax.rsqrt(ms + EPS)) * g


def _row_tile(n, pref):
    t = min(n, pref)
    assert n % t == 0, (n, t)
    return t


def _inproj_kernel(x_ref, g_ref, w_ref, u_ref, q_ref, k_ref, v_ref, kt_ref, vt_ref, h_ref,
                   *, nu, nq, tm, tail_rows, tail_pred):
    i = pl.program_id(0)
    j = pl.program_id(1)

    @pl.when(j == 0)
    def _():
        h_ref[...] = _rms_rows(x_ref[...], g_ref[...]).astype(BF16)

    p = jnp.dot(h_ref[...], w_ref[...], preferred_element_type=F32)
    tu = p.shape[1] // 2

    @pl.when(j < nu)
    def _():
        u_ref[...] = p[:, :tu] * _sigmoid(p[:, tu:])

    @pl.when((j >= nu) & (j < nu + nq))
    def _():
        q_ref[...] = p.astype(BF16)

    is_tail = tail_pred(i)

    @pl.when((j >= nu + nq) & (j < nu + 2 * nq))
    def _():
        k_ref[...] = p.astype(BF16)

        @pl.when(is_tail)
        def _():
            kt_ref[...] = p[tm - tail_rows:, :]

    @pl.when(j >= nu + 2 * nq)
    def _():
        v_ref[...] = p.astype(BF16)

        @pl.when(is_tail)
        def _():
            vt_ref[...] = p[tm - tail_rows:, :]


def _inproj(x, g, w, layer, *, seq_len, d_conv, d_attn, tm_pref=512, tn=1024):
    n, d = x.shape
    tm = _row_tile(n, tm_pref)
    tu = tn // 2
    assert d_conv % tu == 0 and d_attn % tn == 0
    nu, nq = d_conv // tu, d_attn // tn
    ni, nj = n // tm, nu + 3 * nq
    keep = min(BAND_PAST, seq_len)
    if seq_len % tm == 0:
        bps = seq_len // tm
        assert tm % keep == 0 or keep % tm == 0
        tail_rows = min(tm, keep)
        ntb = keep // tail_rows
        n_tail = (n // seq_len) * keep

        def tail_blk(i):
            return (i // bps) * ntb + jnp.clip(i % bps - (bps - ntb), 0, ntb - 1)

        def tail_pred(i):
            return (i % bps) >= (bps - ntb)
    else:
        assert tm % seq_len == 0 and keep == seq_len
        tail_rows, n_tail = tm, n

        def tail_blk(i):
            return i

        def tail_pred(i):
            return i >= 0

    kern = functools.partial(_inproj_kernel, nu=nu, nq=nq, tm=tm, tail_rows=tail_rows, tail_pred=tail_pred)
    col = lambda j, lo: jnp.clip(j - lo, 0, nq - 1)
    return pl.pallas_call(
        kern,
        grid=(ni, nj),
        in_specs=[
            pl.BlockSpec((tm, d), lambda i, j: (i, 0)),
            pl.BlockSpec((None, 1, d), lambda i, j: (layer, 0, 0)),
            pl.BlockSpec((None, d, tn), lambda i, j: (layer, 0, j)),
        ],
        out_specs=[
            pl.BlockSpec((tm, tu), lambda i, j: (i, jnp.minimum(j, nu - 1))),
            pl.BlockSpec((tm, tn), lambda i, j: (i, col(j, nu))),
            pl.BlockSpec((tm, tn), lambda i, j: (i, col(j, nu + nq))),
            pl.BlockSpec((tm, tn), lambda i, j: (i, col(j, nu + 2 * nq))),
            pl.BlockSpec((tail_rows, tn), lambda i, j: (tail_blk(i), col(j, nu + nq))),
            pl.BlockSpec((tail_rows, tn), lambda i, j: (tail_blk(i), col(j, nu + 2 * nq))),
        ],
        out_shape=[
            jax.ShapeDtypeStruct((n, d_conv), F32),
            jax.ShapeDtypeStruct((n, d_attn), BF16),
            jax.ShapeDtypeStruct((n, d_attn), BF16),
            jax.ShapeDtypeStruct((n, d_attn), BF16),
            jax.ShapeDtypeStruct((n_tail, d_attn), F32),
            jax.ShapeDtypeStruct((n_tail, d_attn), F32),
        ],
        scratch_shapes=[pltpu.VMEM((tm, d), BF16)],
        compiler_params=_cparams(2),
        name="inproj",
    )(x, g, w)


def _attend_pair(q2, kb, vb, bias, n_invalid):
    lane = lax.broadcasted_iota(jnp.int32, (CHUNK, PAIR), 1)
    lo = lane < HEAD_DIM
    qs = q2 * jnp.asarray(HEAD_DIM ** -0.5, q2.dtype)
    zero = jnp.zeros_like(qs)
    qs = jnp.concatenate([jnp.where(lo, qs, zero), jnp.where(lo, zero, qs)], axis=0)
    s = lax.dot_general(qs, kb, (((1,), (1,)), ((), ())), preferred_element_type=F32)
    s = s + bias
    if n_invalid is None:
        first = s[:, :LANES]
        col = lax.broadcasted_iota(jnp.int32, first.shape, 1)
        s = jnp.concatenate([jnp.where(col >= CHUNK, first, NEG), s[:, LANES:]], axis=1)
    else:
        col = lax.broadcasted_iota(jnp.int32, s.shape, 1)
        s = jnp.where(col >= n_invalid, s, NEG)
    m = jnp.max(s, axis=-1, keepdims=True)
    e = jnp.exp(s - m)
    l = jnp.sum(e, axis=-1, keepdims=True)
    r = jnp.dot(e.astype(BF16), vb, preferred_element_type=F32)
    r = r * (1.0 / l)
    return jnp.where(lo, r[:CHUNK], r[CHUNK:])


def _attn_prompt_kernel(q_ref, k_ref, v_ref, bias_ref, o_ref, kp_ref, vp_ref, *, seq_len, npairs):
    pad = BAND_PAD - CHUNK
    kp_ref[0:pad, :] = jnp.zeros((pad, kp_ref.shape[1]), BF16)
    vp_ref[0:pad, :] = jnp.zeros((pad, vp_ref.shape[1]), BF16)
    kp_ref[pad:, :] = k_ref[...]
    vp_ref[pad:, :] = v_ref[...]

    def chunk(c, masked):
        r0 = pl.multiple_of(c * CHUNK, CHUNK)
        n_invalid = (PAST_CHUNKS + 1 - c) * CHUNK if masked else None
        for hp in range(npairs):
            cols = slice(hp * PAIR, (hp + 1) * PAIR)
            o2 = _attend_pair(q_ref[pl.ds(r0, CHUNK), cols],
                              kp_ref[pl.ds(r0, BAND_PAD), cols],
                              vp_ref[pl.ds(r0, BAND_PAD), cols],
                              bias_ref[hp], n_invalid)
            o_ref[pl.ds(r0, CHUNK), cols] = o2.astype(o_ref.dtype)

    n_chunks = seq_len // CHUNK
    n_masked = min(PAST_CHUNKS, n_chunks)

    def masked_body(c, carry):
        chunk(c, True)
        return carry

    def plain_body(c, carry):
        chunk(c, False)
        return carry

    lax.fori_loop(0, n_masked, masked_body, 0)
    lax.fori_loop(n_masked, n_chunks, plain_body, 0)


def _attn_prompt(q, k, v, bias, *, seq_len):
    n, da = q.shape
    b = n // seq_len
    npairs = da // PAIR
    kern = functools.partial(_attn_prompt_kernel, seq_len=seq_len, npairs=npairs)
    row = pl.BlockSpec((seq_len, da), lambda i: (i, 0))
    return pl.pallas_call(
        kern,
        grid=(b,),
        in_specs=[row, row, row, pl.BlockSpec((npairs, PAIR, BAND_PAD), lambda i: (0, 0, 0))],
        out_specs=row,
        out_shape=jax.ShapeDtypeStruct((n, da), BF16),
        scratch_shapes=[pltpu.VMEM((seq_len + BAND_PAD - CHUNK, da), BF16),
                        pltpu.VMEM((seq_len + BAND_PAD - CHUNK, da), BF16)],
        compiler_params=_cparams(1),
        name="attn_prompt",
    )(q, k, v, bias)


def _attn_sample_kernel(q_ref, k_ref, v_ref, kt_ref, vt_ref, ck_ref, cv_ref, bias_ref,
                        o_ref, nk_ref, nv_ref, kb_ref, vb_ref, *, npairs):
    keep = ck_ref.shape[0]
    da = ck_ref.shape[1]
    kb_ref[0:CHUNK, :] = jnp.zeros((CHUNK, da), BF16)
    vb_ref[0:CHUNK, :] = jnp.zeros((CHUNK, da), BF16)
    kb_ref[CHUNK:CHUNK + keep, :] = ck_ref[...].astype(BF16)
    vb_ref[CHUNK:CHUNK + keep, :] = cv_ref[...].astype(BF16)
    kb_ref[CHUNK + keep:, :] = k_ref[...]
    vb_ref[CHUNK + keep:, :] = v_ref[...]
    nk_ref[0:keep - CHUNK, :] = ck_ref[CHUNK:, :]
    nv_ref[0:keep - CHUNK, :] = cv_ref[CHUNK:, :]
    nk_ref[keep - CHUNK:, :] = kt_ref[...]
    nv_ref[keep - CHUNK:, :] = vt_ref[...]
    for hp in range(npairs):
        cols = slice(hp * PAIR, (hp + 1) * PAIR)
        o2 = _attend_pair(q_ref[:, cols], kb_ref[:, cols], vb_ref[:, cols], bias_ref[hp], None)
        o_ref[:, cols] = o2.astype(o_ref.dtype)


def _attn_sample(q, k, v, kt, vt, cache_k, cache_v, bias, layer):
    n, da = q.shape
    b = n // CHUNK
    keep = cache_k.shape[2]
    assert keep == BAND_PAST
    npairs = da // PAIR
    kern = functools.partial(_attn_sample_kernel, npairs=npairs)
    row = pl.BlockSpec((CHUNK, da), lambda i: (i, 0))
    cache = pl.BlockSpec((None, None, keep, da), lambda i: (layer, i, 0, 0))
    newc = pl.BlockSpec((None, keep, da), lambda i: (i, 0, 0))
    return pl.pallas_call(
        kern,
        grid=(b,),
        in_specs=[row, row, row, row, row, cache, cache,
                  pl.BlockSpec((npairs, PAIR, BAND_PAD), lambda i: (0, 0, 0))],
        out_specs=[row, newc, newc],
        out_shape=[jax.ShapeDtypeStruct((n, da), BF16),
                   jax.ShapeDtypeStruct((b, keep, da), F32),
                   jax.ShapeDtypeStruct((b, keep, da), F32)],
        scratch_shapes=[pltpu.VMEM((BAND_PAD, da), BF16), pltpu.VMEM((BAND_PAD, da), BF16)],
        compiler_params=_cparams(1),
        name="attn_sample",
    )(q, k, v, kt, vt, cache_k, cache_v, bias)


def _band_bias(rel_bias):
    h = rel_bias.shape[0]
    i = jnp.arange(CHUNK)[:, None]
    j = jnp.arange(BAND_PAD)[None, :]
    rel = jnp.clip(i - j + BAND, -REL_CLIP, REL_CLIP) + REL_CLIP
    return rel_bias[:, rel].reshape(h // 2, PAIR, BAND_PAD).astype(F32)


CONV_ROWS = 32


def _conv_kernel(u_ref, halo_ref, pre_ref, cw_ref, cb_ref, lg_ref, lb_ref, o_ref, ext_ref, y_ref, *, tt):
    t = pl.program_id(1)
    c = u_ref.shape[1]

    @pl.when(t == 0)
    def _():
        ext_ref[0:CONV_HALO, :] = pre_ref[...]

    @pl.when(t > 0)
    def _():
        ext_ref[0:CONV_HALO, :] = halo_ref[...]

    ext_ref[CONV_HALO:, :] = u_ref[...]
    first = CONV_HALO - (CONV_WIDTH - 1)

    for r in range(0, tt, CONV_ROWS):
        acc = jnp.broadcast_to(cb_ref[...], (CONV_ROWS, c))
        for w in range(CONV_WIDTH):
            acc = acc + ext_ref[r + first + w:r + first + w + CONV_ROWS, :] * cw_ref[w:w + 1, :]
        y_ref[r:r + CONV_ROWS, :] = acc
    y = y_ref[...]
    mu = jnp.mean(y, axis=-1, keepdims=True)
    d = y - mu
    var = jnp.mean(d * d, axis=-1, keepdims=True)
    z = (d * lax.rsqrt(var + EPS)) * lg_ref[...] + lb_ref[...]
    o_ref[...] = (z * _sigmoid(z)).astype(o_ref.dtype)


def _conv_module(u, prefix, conv_w, conv_b, ln_g, ln_b, layer, *, seq_len, tt_pref=256):
    n, c = u.shape
    b = n // seq_len
    tt = _row_tile(seq_len, tt_pref)
    assert tt % CONV_ROWS == 0
    nt = seq_len // tt
    hb = tt // CONV_HALO
    kern = functools.partial(_conv_kernel, tt=tt)
    vec = pl.BlockSpec((None, 1, c), lambda i, t: (layer, 0, 0))
    return pl.pallas_call(
        kern,
        grid=(b, nt),
        in_specs=[
            pl.BlockSpec((tt, c), lambda i, t: (i * nt + t, 0)),
            pl.BlockSpec((CONV_HALO, c), lambda i, t: (jnp.maximum((i * nt + t) * hb - 1, 0), 0)),
            pl.BlockSpec((None, CONV_HALO, c), lambda i, t: (i, 0, 0)),
            pl.BlockSpec((None, CONV_WIDTH, c), lambda i, t: (layer, 0, 0)),
            vec, vec, vec,
        ],
        out_specs=pl.BlockSpec((tt, c), lambda i, t: (i * nt + t, 0)),
        out_shape=jax.ShapeDtypeStruct((n, c), BF16),
        scratch_shapes=[pltpu.VMEM((tt + CONV_HALO, c), F32), pltpu.VMEM((tt, c), F32)],
        compiler_params=_cparams(2),
        name="conv_module",
    )(u, u, prefix, conv_w, conv_b, ln_g, ln_b)


def _merge_kernel(x_ref, c_ref, a_ref, g_ref, wco_ref, wo_ref, wg1_ref, wg2_ref, b1_ref, b2_ref,
                  wout_ref, xt_ref, o_ref, h_ref, m_ref, *, nc, tn):
    p = pl.program_id(1)

    @pl.when(p == 0)
    def _():
        h_ref[...] = _rms_rows(x_ref[...], g_ref[...]).astype(BF16)

    @pl.when(p < nc)
    def _():
        h = h_ref[...]
        gc = _sigmoid(jnp.dot(h, wg1_ref[...], preferred_element_type=F32) + b1_ref[...])
        ga = _sigmoid(jnp.dot(h, wg2_ref[...], preferred_element_type=F32) + b2_ref[...])
        yc = jnp.dot(c_ref[...], wco_ref[...], preferred_element_type=F32)
        ya = jnp.dot(a_ref[...], wo_ref[...], preferred_element_type=F32)
        m_ref[p] = (gc * yc + ga * ya).astype(BF16)

    @pl.when(p >= nc)
    def _():
        acc = xt_ref[...]
        for cc in range(nc):
            acc = acc + jnp.dot(m_ref[cc], wout_ref[cc * tn:(cc + 1) * tn, :], preferred_element_type=F32)
        o_ref[...] = acc


def _merge(x, cb, ab, g, wco, wo, wg, bg, wout, layer, *, tm_pref=512, tn=512):
    n, d = x.shape
    dc, da = cb.shape[1], ab.shape[1]
    tm = _row_tile(n, tm_pref)
    assert d % tn == 0
    nc = d // tn
    kern = functools.partial(_merge_kernel, nc=nc, tn=tn)
    ca = lambda p: jnp.minimum(p, nc - 1)
    cb_ = lambda p: jnp.maximum(p - nc, 0)
    return pl.pallas_call(
        kern,
        grid=(n // tm, 2 * nc),
        in_specs=[
            pl.BlockSpec((tm, d), lambda i, p: (i, 0)),
            pl.BlockSpec((tm, dc), lambda i, p: (i, 0)),
            pl.BlockSpec((tm, da), lambda i, p: (i, 0)),
            pl.BlockSpec((None, 1, d), lambda i, p: (layer, 0, 0)),
            pl.BlockSpec((None, dc, tn), lambda i, p: (layer, 0, ca(p))),
            pl.BlockSpec((None, da, tn), lambda i, p: (layer, 0, ca(p))),
            pl.BlockSpec((None, d, tn), lambda i, p: (layer, 0, ca(p))),
            pl.BlockSpec((None, d, tn), lambda i, p: (layer, 0, nc + ca(p))),
            pl.BlockSpec((None, 1, tn), lambda i, p: (layer, 0, ca(p))),
            pl.BlockSpec((None, 1, tn), lambda i, p: (layer, 0, nc + ca(p))),
            pl.BlockSpec((None, d, tn), lambda i, p: (layer, 0, cb_(p))),
            pl.BlockSpec((tm, tn), lambda i, p: (i, cb_(p))),
        ],
        out_specs=pl.BlockSpec((tm, tn), lambda i, p: (i, cb_(p))),
        out_shape=jax.ShapeDtypeStruct((n, d), F32),
        scratch_shapes=[pltpu.VMEM((tm, d), BF16), pltpu.VMEM((nc, tm, tn), BF16)],
        compiler_params=_cparams(2),
        name="merge",
    )(x, cb, ab, g, wco, wo, wg, wg, bg, bg, wout, x)


def _ffn_kernel(x_ref, g_ref, wup_ref, wdn_ref, gf_ref, o_ref, h_ref, *, nf, final_norm):
    j = pl.program_id(1)

    @pl.when(j == 0)
    def _():
        x = x_ref[...]
        h_ref[...] = _rms_rows(x, g_ref[...]).astype(BF16)
        o_ref[...] = x

    gu = jnp.dot(h_ref[...], wup_ref[...], preferred_element_type=F32)
    tf = gu.shape[1] // 2
    gate, up = gu[:, :tf], gu[:, tf:]
    act = ((gate * _sigmoid(gate)) * up).astype(BF16)
    o_ref[...] += jnp.dot(act, wdn_ref[...], preferred_element_type=F32)

    if final_norm:
        @pl.when(j == nf - 1)
        def _():
            o_ref[...] = _rms_rows(o_ref[...], gf_ref[...])


def _ffn(x, g, wup, wdn, gf, layer, *, final_norm, tm_pref=512, tf=512):
    n, d = x.shape
    dff = wdn.shape[1]
    tm = _row_tile(n, tm_pref)
    assert dff % tf == 0
    nf = dff // tf
    kern = functools.partial(_ffn_kernel, nf=nf, final_norm=final_norm)
    return pl.pallas_call(
        kern,
        grid=(n // tm, nf),
        in_specs=[
            pl.BlockSpec((tm, d), lambda i, j: (i, 0)),
            pl.BlockSpec((None, 1, d), lambda i, j: (layer, 0, 0)),
            pl.BlockSpec((None, d, 2 * tf), lambda i, j: (layer, 0, j)),
            pl.BlockSpec((None, tf, d), lambda i, j: (layer, j, 0)),
            pl.BlockSpec((1, d), lambda i, j: (0, 0)),
        ],
        out_specs=pl.BlockSpec((tm, d), lambda i, j: (i, 0)),
        out_shape=jax.ShapeDtypeStruct((n, d), F32),
        scratch_shapes=[pltpu.VMEM((tm, d), BF16)],
        compiler_params=_cparams(2),
        name="ffn",
    )(x, g, wup, wdn, gf)


def _interleave_cols(wa, wb, tile):
    l, k, c = wa.shape
    a = wa.reshape(l, k, c // tile, tile)
    b = wb.reshape(l, k, c // tile, tile)
    return jnp.stack([a, b], axis=3).reshape(l, k, 2 * c)


def kernel(x_prompt, x_sample, cache_k, cache_v, state_conv, norm_mix, w_in, conv_w, conv_b, ln_g, ln_b,
           w_conv_out, rel_bias, w_o, w_gate, b_gate, w_out, norm_ffn, w_up, w_down, norm_final):
    depth = w_in.shape[0]
    bp, tp, d = x_prompt.shape
    bs, ts, _ = x_sample.shape
    dc = conv_w.shape[2]
    da = w_o.shape[1]
    dff = w_down.shape[1]
    keep_s = cache_k.shape[2]
    assert ts == CHUNK and tp % CHUNK == 0
    tn_in, tf = 1024 if dc % 512 == 0 and da % 1024 == 0 else 256, 512 if dff % 512 == 0 else 256

    w_in_p = jnp.concatenate(
        [_interleave_cols(w_in[:, :, :dc], w_in[:, :, dc:2 * dc], tn_in // 2), w_in[:, :, 2 * dc:]],
        axis=2).astype(BF16)
    w_up_p = _interleave_cols(w_up[:, :, :dff], w_up[:, :, dff:], tf).astype(BF16)
    w_down_b = w_down.astype(BF16)
    w_co_b = w_conv_out.astype(BF16)
    w_o_b = w_o.astype(BF16)
    w_gate_b = w_gate.astype(BF16)
    w_out_b = w_out.astype(BF16)
    row3 = lambda a: a.reshape(depth, 1, a.shape[-1])
    g_mix, g_ffn, cb3, lg3, lb3, bg3 = map(row3, (norm_mix, norm_ffn, conv_b, ln_g, ln_b, b_gate))
    g_fin = norm_final.reshape(1, d)
    ck = cache_k.reshape(depth, bs, keep_s, da)
    cv = cache_v.reshape(depth, bs, keep_s, da)
    pad_front = CONV_HALO - (CONV_WIDTH - 1)
    pre_p = jnp.zeros((bp, CONV_HALO, dc), F32)

    xp = x_prompt.reshape(bp * tp, d)
    xs = x_sample.reshape(bs * ts, d)
    keep_p = min(BAND_PAST, tp)
    kp, vp, cp, ksl, vsl, csl = [], [], [], [], [], []
    for l in range(depth):
        bias = _band_bias(rel_bias[l])
        merge_w = (g_mix, w_co_b, w_o_b, w_gate_b, bg3, w_out_b, l)
        last = l == depth - 1
        u, q, k, v, kt, vt = _inproj(xp, g_mix, w_in_p, l, seq_len=tp, d_conv=dc, d_attn=da, tn=tn_in)
        att = _attn_prompt(q, k, v, bias, seq_len=tp)
        cact = _conv_module(u, pre_p, conv_w, cb3, lg3, lb3, l, seq_len=tp)
        xp = _merge(xp, cact, att, *merge_w)
        xp = _ffn(xp, g_ffn, w_up_p, w_down_b, g_fin, l, final_norm=last, tf=tf)
        kp.append(kt.reshape(bp, keep_p, da // HEAD_DIM, HEAD_DIM))
        vp.append(vt.reshape(bp, keep_p, da // HEAD_DIM, HEAD_DIM))
        cp.append(u.reshape(bp, tp, dc)[:, tp - (CONV_WIDTH - 1):])
        u, q, k, v, kt, vt = _inproj(xs, g_mix, w_in_p, l, seq_len=ts, d_conv=dc, d_attn=da, tn=tn_in)
        att, nk, nv = _attn_sample(q, k, v, kt, vt, ck, cv, bias, l)
        pre_s = jnp.pad(state_conv[l], ((0, 0), (pad_front, 0), (0, 0)))
        cact = _conv_module(u, pre_s, conv_w, cb3, lg3, lb3, l, seq_len=ts)
        xs = _merge(xs, cact, att, *merge_w)
        xs = _ffn(xs, g_ffn, w_up_p, w_down_b, g_fin, l, final_norm=last, tf=tf)
        ksl.append(nk.reshape(bs, keep_s, da // HEAD_DIM, HEAD_DIM))
        vsl.append(nv.reshape(bs, keep_s, da // HEAD_DIM, HEAD_DIM))
        csl.append(u.reshape(bs, ts, dc)[:, ts - (CONV_WIDTH - 1):])
    return (xp.reshape(bp, tp, d), xs.reshape(bs, ts, d), jnp.stack(kp), jnp.stack(vp), jnp.stack(cp),
            jnp.stack(ksl), jnp.stack(vsl), jnp.stack(csl))
```

```python
import functools

import jax
import jax.numpy as jnp
from jax import lax
from jax.experimental import pallas as pl
from jax.experimental.pallas import tpu as pltpu

F32 = jnp.float32
BF16 = jnp.bfloat16

CHUNK = 64
PAST_CHUNKS = 8
BAND_PAST = PAST_CHUNKS * CHUNK
BAND = BAND_PAST + CHUNK
BAND_PAD = BAND + CHUNK
HEAD_DIM = 64
CONV_WIDTH = 31
CONV_HALO = 32
REL_CLIP = 128
EPS = 1e-6
NEG = -1e30

LANES = 128
SUBLANES = 8
PAIR = 2 * HEAD_DIM
VMEM_LIMIT_BYTES = 56 * 1024 * 1024


def _cparams(ndims):
    return pltpu.CompilerParams(dimension_semantics=("arbitrary",) * ndims,
                                vmem_limit_bytes=VMEM_LIMIT_BYTES)


def _sigmoid(x):
    return 1.0 / (1.0 + jnp.exp(-x))


def _rms_rows(x, g):
    ms = jnp.mean(x * x, axis=-1, keepdims=True)
    return (x * lax.rsqrt(ms + EPS)) * g


def _row_tile(n, pref):
    t = min(n, pref)
    assert n % t == 0, (n, t)
    return t


def _inproj_kernel(x_ref, g_ref, wa_ref, wb_ref, wq_ref, h_ref, u_ref, q_ref, k_ref, v_ref, kt_ref, vt_ref,
                   *, nu, nq, tm, tail_rows, tail_pred):
    i = pl.program_id(0)
    j = pl.program_id(1)

    @pl.when(j == 0)
    def _():
        h_ref[...] = _rms_rows(x_ref[...], g_ref[...]).astype(BF16)

    @pl.when(j < nu)
    def _():
        h = h_ref[...]
        a = jnp.dot(h, wa_ref[...], preferred_element_type=F32)
        b = jnp.dot(h, wb_ref[...], preferred_element_type=F32)
        u_ref[...] = a * _sigmoid(b)

    is_tail = tail_pred(i)

    @pl.when(j >= nu)
    def _():
        p = jnp.dot(h_ref[...], wq_ref[...], preferred_element_type=F32)
        pb = p.astype(BF16)

        @pl.when(j < nu + nq)
        def _():
            q_ref[...] = pb

        @pl.when((j >= nu + nq) & (j < nu + 2 * nq))
        def _():
            k_ref[...] = pb

            @pl.when(is_tail)
            def _():
                kt_ref[...] = p[tm - tail_rows:, :]

        @pl.when(j >= nu + 2 * nq)
        def _():
            v_ref[...] = pb

            @pl.when(is_tail)
            def _():
                vt_ref[...] = p[tm - tail_rows:, :]


def _inproj(x, g, w, layer, *, seq_len, d_conv, d_attn, tm_pref=512, tn=512):
    n, d = x.shape
    tm = _row_tile(n, tm_pref)
    assert d_conv % tn == 0 and d_attn % tn == 0
    nu, nq = d_conv // tn, d_attn // tn
    ni, nj = n // tm, nu + 3 * nq
    keep = min(BAND_PAST, seq_len)
    if seq_len % tm == 0:
        bps = seq_len // tm
        assert tm % keep == 0 or keep % tm == 0
        tail_rows = min(tm, keep)
        ntb = keep // tail_rows
        n_tail = (n // seq_len) * keep

        def tail_blk(i):
            return (i // bps) * ntb + jnp.clip(i % bps - (bps - ntb), 0, ntb - 1)

        def tail_pred(i):
            return (i % bps) >= (bps - ntb)
    else:
        assert tm % seq_len == 0 and keep == seq_len
        tail_rows, n_tail = tm, n

        def tail_blk(i):
            return i

        def tail_pred(i):
            return i >= 0

    kern = functools.partial(_inproj_kernel, nu=nu, nq=nq, tm=tm, tail_rows=tail_rows, tail_pred=tail_pred)
    ucol = lambda j: jnp.minimum(j, nu - 1)
    col = lambda j, lo: jnp.clip(j - lo, 0, nq - 1)
    return pl.pallas_call(
        kern,
        grid=(ni, nj),
        in_specs=[
            pl.BlockSpec((tm, d), lambda i, j: (i, 0)),
            pl.BlockSpec((None, 1, d), lambda i, j: (layer, 0, 0)),
            pl.BlockSpec((None, d, tn), lambda i, j: (layer, 0, ucol(j))),
            pl.BlockSpec((None, d, tn), lambda i, j: (layer, 0, nu + ucol(j))),
            pl.BlockSpec((None, d, tn), lambda i, j: (layer, 0, 2 * nu + jnp.clip(j - nu, 0, 3 * nq - 1))),
        ],
        out_specs=[
            pl.BlockSpec((tm, d), lambda i, j: (i, 0)),
            pl.BlockSpec((tm, tn), lambda i, j: (i, ucol(j))),
            pl.BlockSpec((tm, tn), lambda i, j: (i, col(j, nu))),
            pl.BlockSpec((tm, tn), lambda i, j: (i, col(j, nu + nq))),
            pl.BlockSpec((tm, tn), lambda i, j: (i, col(j, nu + 2 * nq))),
            pl.BlockSpec((tail_rows, tn), lambda i, j: (tail_blk(i), col(j, nu + nq))),
            pl.BlockSpec((tail_rows, tn), lambda i, j: (tail_blk(i), col(j, nu + 2 * nq))),
        ],
        out_shape=[
            jax.ShapeDtypeStruct((n, d), BF16),
            jax.ShapeDtypeStruct((n, d_conv), F32),
            jax.ShapeDtypeStruct((n, d_attn), BF16),
            jax.ShapeDtypeStruct((n, d_attn), BF16),
            jax.ShapeDtypeStruct((n, d_attn), BF16),
            jax.ShapeDtypeStruct((n_tail, d_attn), F32),
            jax.ShapeDtypeStruct((n_tail, d_attn), F32),
        ],
        compiler_params=_cparams(2),
        name="inproj",
    )(x, g, w, w, w)


def _pair_scores(q2, kb):
    lane = lax.broadcasted_iota(jnp.int32, (CHUNK, PAIR), 1)
    lo = lane < HEAD_DIM
    qs = q2 * jnp.asarray(HEAD_DIM ** -0.5, q2.dtype)
    zero = jnp.zeros_like(qs)
    qs = jnp.concatenate([jnp.where(lo, qs, zero), jnp.where(lo, zero, qs)], axis=0)
    return lax.dot_general(qs, kb, (((1,), (1,)), ((), ())), preferred_element_type=F32)


def _pair_softmax_pv(s, bias, vb, n_invalid):
    s = s + bias
    if n_invalid is None:
        first = s[:, :LANES]
        col = lax.broadcasted_iota(jnp.int32, first.shape, 1)
        s = jnp.concatenate([jnp.where(col >= CHUNK, first, NEG), s[:, LANES:]], axis=1)
    else:
        col = lax.broadcasted_iota(jnp.int32, s.shape, 1)
        s = jnp.where(col >= n_invalid, s, NEG)
    m = jnp.max(s, axis=-1, keepdims=True)
    e = jnp.exp(s - m)
    l = jnp.sum(e, axis=-1, keepdims=True)
    r = jnp.dot(e.astype(BF16), vb, preferred_element_type=F32)
    r = r * (1.0 / l)
    lane = lax.broadcasted_iota(jnp.int32, (CHUNK, PAIR), 1)
    return jnp.where(lane < HEAD_DIM, r[:CHUNK], r[CHUNK:])


def _attn_prompt_kernel(q_ref, k_ref, v_ref, bias_ref, o_ref, kp_ref, vp_ref, s_ref, *, seq_len, npairs):
    pad = BAND_PAD - CHUNK
    kp_ref[0:pad, :] = jnp.zeros((pad, kp_ref.shape[1]), BF16)
    vp_ref[0:pad, :] = jnp.zeros((pad, vp_ref.shape[1]), BF16)
    kp_ref[pad:, :] = k_ref[...]
    vp_ref[pad:, :] = v_ref[...]
    n_chunks = seq_len // CHUNK
    n_masked = min(PAST_CHUNKS, n_chunks)
    assert n_chunks % 2 == 0 and n_masked % 2 == 0

    def scores(c, slot):
        r0 = pl.multiple_of(c * CHUNK, CHUNK)
        for hp in range(npairs):
            cols = slice(hp * PAIR, (hp + 1) * PAIR)
            s_ref[slot, hp] = _pair_scores(q_ref[pl.ds(r0, CHUNK), cols], kp_ref[pl.ds(r0, BAND_PAD), cols])

    def finish(c, slot, masked):
        r0 = pl.multiple_of(c * CHUNK, CHUNK)
        n_invalid = (PAST_CHUNKS + 1 - c) * CHUNK if masked else None
        for hp in range(npairs):
            cols = slice(hp * PAIR, (hp + 1) * PAIR)
            o2 = _pair_softmax_pv(s_ref[slot, hp], bias_ref[hp], vp_ref[pl.ds(r0, BAND_PAD), cols], n_invalid)
            o_ref[pl.ds(r0, CHUNK), cols] = o2.astype(o_ref.dtype)

    def two_chunks(cc, masked):
        c = 2 * cc
        scores(c + 1, 1)
        finish(c, 0, masked)
        scores(jnp.minimum(c + 2, n_chunks - 1), 0)
        finish(c + 1, 1, masked)

    def masked_body(cc, carry):
        two_chunks(cc, True)
        return carry

    def plain_body(cc, carry):
        two_chunks(cc, False)
        return carry

    scores(0, 0)
    lax.fori_loop(0, n_masked // 2, masked_body, 0)
    lax.fori_loop(n_masked // 2, n_chunks // 2, plain_body, 0)


def _attn_prompt(q, k, v, bias, layer, *, seq_len, group_pairs=4):
    n, da = q.shape
    b = n // seq_len
    gp = min(group_pairs, da // PAIR)
    gw = gp * PAIR
    assert da % gw == 0
    kern = functools.partial(_attn_prompt_kernel, seq_len=seq_len, npairs=gp)
    row = pl.BlockSpec((seq_len, gw), lambda i, g: (i, g))
    return pl.pallas_call(
        kern,
        grid=(b, da // gw),
        in_specs=[row, row, row, pl.BlockSpec((None, gp, PAIR, BAND_PAD), lambda i, g: (layer, g, 0, 0))],
        out_specs=row,
        out_shape=jax.ShapeDtypeStruct((n, da), BF16),
        scratch_shapes=[pltpu.VMEM((seq_len + BAND_PAD - CHUNK, gw), BF16),
                        pltpu.VMEM((seq_len + BAND_PAD - CHUNK, gw), BF16),
                        pltpu.VMEM((2, gp, PAIR, BAND_PAD), F32)],
        compiler_params=_cparams(2),
        name="attn_prompt",
    )(q, k, v, bias)


def _attn_sample_kernel(q_ref, k_ref, v_ref, kt_ref, vt_ref, ck_ref, cv_ref, bias_ref,
                        o_ref, nk_ref, nv_ref, kb_ref, vb_ref, *, npairs):
    keep = ck_ref.shape[0]
    da = ck_ref.shape[1]
    kb_ref[0:CHUNK, :] = jnp.zeros((CHUNK, da), BF16)
    vb_ref[0:CHUNK, :] = jnp.zeros((CHUNK, da), BF16)
    kb_ref[CHUNK:CHUNK + keep, :] = ck_ref[...].astype(BF16)
    vb_ref[CHUNK:CHUNK + keep, :] = cv_ref[...].astype(BF16)
    kb_ref[CHUNK + keep:, :] = k_ref[...]
    vb_ref[CHUNK + keep:, :] = v_ref[...]
    nk_ref[0:keep - CHUNK, :] = ck_ref[CHUNK:, :]
    nv_ref[0:keep - CHUNK, :] = cv_ref[CHUNK:, :]
    nk_ref[keep - CHUNK:, :] = kt_ref[...]
    nv_ref[keep - CHUNK:, :] = vt_ref[...]
    cols = [slice(hp * PAIR, (hp + 1) * PAIR) for hp in range(npairs)]
    ss = [_pair_scores(q_ref[:, c], kb_ref[:, c]) for c in cols]
    for hp, c in enumerate(cols):
        o_ref[:, c] = _pair_softmax_pv(ss[hp], bias_ref[hp], vb_ref[:, c], None).astype(o_ref.dtype)


def _attn_sample(q, k, v, kt, vt, cache_k, cache_v, bias, layer):
    n, da = q.shape
    b = n // CHUNK
    keep = cache_k.shape[2]
    assert keep == BAND_PAST
    npairs = da // PAIR
    kern = functools.partial(_attn_sample_kernel, npairs=npairs)
    row = pl.BlockSpec((CHUNK, da), lambda i: (i, 0))
    cache = pl.BlockSpec((None, None, keep, da), lambda i: (layer, i, 0, 0))
    newc = pl.BlockSpec((None, keep, da), lambda i: (i, 0, 0))
    return pl.pallas_call(
        kern,
        grid=(b,),
        in_specs=[row, row, row, row, row, cache, cache,
                  pl.BlockSpec((None, npairs, PAIR, BAND_PAD), lambda i: (layer, 0, 0, 0))],
        out_specs=[row, newc, newc],
        out_shape=[jax.ShapeDtypeStruct((n, da), BF16),
                   jax.ShapeDtypeStruct((b, keep, da), F32),
                   jax.ShapeDtypeStruct((b, keep, da), F32)],
        scratch_shapes=[pltpu.VMEM((BAND_PAD, da), BF16), pltpu.VMEM((BAND_PAD, da), BF16)],
        compiler_params=_cparams(1),
        name="attn_sample",
    )(q, k, v, kt, vt, cache_k, cache_v, bias)


def _band_bias(rel_bias):
    depth, h, _ = rel_bias.shape
    n_far = BAND + CHUNK - 1 - REL_CLIP + 1
    n_near = BAND_PAD + CHUNK - 1 - n_far
    far = jnp.broadcast_to(rel_bias[:, :, 2 * REL_CLIP:], (depth, h, n_far))
    near = lax.rev(rel_bias[:, :, 2 * REL_CLIP - n_near:2 * REL_CLIP], (2,))
    g = jnp.concatenate([far, near], axis=2)
    rows = [g[:, :, CHUNK - 1 - i:CHUNK - 1 - i + BAND_PAD] for i in range(CHUNK)]
    return jnp.stack(rows, axis=2).reshape(depth, h // 2, PAIR, BAND_PAD).astype(F32)


CONV_ROWS = 32


def _conv_kernel(u_ref, halo_ref, pre_ref, cw_ref, cb_ref, lg_ref, lb_ref, o_ref, ext_ref, y_ref, *, tt):
    t = pl.program_id(1)
    c = u_ref.shape[1]

    @pl.when(t == 0)
    def _():
        ext_ref[0:CONV_HALO, :] = pre_ref[...]

    @pl.when(t > 0)
    def _():
        ext_ref[0:CONV_HALO, :] = halo_ref[...]

    ext_ref[CONV_HALO:, :] = u_ref[...]
    first = CONV_HALO - (CONV_WIDTH - 1)
    ext = ext_ref[...]
    erows = tt + CONV_HALO
    for s in range(SUBLANES):
        shifted = ext if s == 0 else pltpu.roll(ext, erows - s, 0)
        taps = [(q * SUBLANES, q * SUBLANES + s - first) for q in range(CONV_HALO // SUBLANES + 1)
                if 0 <= q * SUBLANES + s - first < CONV_WIDTH]
        for r in range(0, tt, CONV_ROWS):
            acc = jnp.broadcast_to(cb_ref[...], (CONV_ROWS, c)) if s == 0 else y_ref[r:r + CONV_ROWS, :]
            for off, w in taps:
                acc = acc + shifted[r + off:r + off + CONV_ROWS, :] * cw_ref[w:w + 1, :]
            y_ref[r:r + CONV_ROWS, :] = acc
    y = y_ref[...]
    mu = jnp.mean(y, axis=-1, keepdims=True)
    d = y - mu
    var = jnp.mean(d * d, axis=-1, keepdims=True)
    z = (d * lax.rsqrt(var + EPS)) * lg_ref[...] + lb_ref[...]
    o_ref[...] = (z * _sigmoid(z)).astype(o_ref.dtype)


def _conv_module(u, prefix, conv_w, conv_b, ln_g, ln_b, layer, *, seq_len, tt_pref=256):
    n, c = u.shape
    b = n // seq_len
    tt = _row_tile(seq_len, tt_pref)
    assert tt % CONV_ROWS == 0
    nt = seq_len // tt
    hb = tt // CONV_HALO
    kern = functools.partial(_conv_kernel, tt=tt)
    vec = pl.BlockSpec((None, 1, c), lambda i, t: (layer, 0, 0))
    return pl.pallas_call(
        kern,
        grid=(b, nt),
        in_specs=[
            pl.BlockSpec((tt, c), lambda i, t: (i * nt + t, 0)),
            pl.BlockSpec((CONV_HALO, c), lambda i, t: (jnp.maximum((i * nt + t) * hb - 1, 0), 0)),
            pl.BlockSpec((None, CONV_HALO, c), lambda i, t: (i, 0, 0)),
            pl.BlockSpec((None, CONV_WIDTH, c), lambda i, t: (layer, 0, 0)),
            vec, vec, vec,
        ],
        out_specs=pl.BlockSpec((tt, c), lambda i, t: (i * nt + t, 0)),
        out_shape=jax.ShapeDtypeStruct((n, c), BF16),
        scratch_shapes=[pltpu.VMEM((tt + CONV_HALO, c), F32), pltpu.VMEM((tt, c), F32)],
        compiler_params=_cparams(2),
        name="conv_module",
    )(u, u, prefix, conv_w, conv_b, ln_g, ln_b)


def _merge_kernel(h_ref, c_ref, a_ref, wco_ref, wo_ref, wg1_ref, wg2_ref, b1_ref, b2_ref,
                  wout_ref, xt_ref, o_ref, m_ref, *, nc, tn):
    p = pl.program_id(1)

    @pl.when(p < nc)
    def _():
        h = h_ref[...]
        gc = _sigmoid(jnp.dot(h, wg1_ref[...], preferred_element_type=F32) + b1_ref[...])
        yc = jnp.dot(c_ref[...], wco_ref[...], preferred_element_type=F32)
        mc = gc * yc
        ga = _sigmoid(jnp.dot(h, wg2_ref[...], preferred_element_type=F32) + b2_ref[...])
        ya = jnp.dot(a_ref[...], wo_ref[...], preferred_element_type=F32)
        m_ref[p] = (mc + ga * ya).astype(BF16)

    @pl.when(p >= nc)
    def _():
        acc = xt_ref[...]
        for cc in range(nc):
            acc = acc + jnp.dot(m_ref[cc], wout_ref[cc * tn:(cc + 1) * tn, :], preferred_element_type=F32)
        o_ref[...] = acc


def _merge(x, h, cb, ab, wco, wo, wg, bg, wout, layer, *, tm_pref=1024, tn=512):
    n, d = x.shape
    dc, da = cb.shape[1], ab.shape[1]
    tm = _row_tile(n, tm_pref)
    assert d % tn == 0
    nc = d // tn
    kern = functools.partial(_merge_kernel, nc=nc, tn=tn)
    ca = lambda p: jnp.minimum(p, nc - 1)
    cb_ = lambda p: jnp.maximum(p - nc, 0)
    return pl.pallas_call(
        kern,
        grid=(n // tm, 2 * nc),
        in_specs=[
            pl.BlockSpec((tm, d), lambda i, p: (i, 0)),
            pl.BlockSpec((tm, dc), lambda i, p: (i, 0)),
            pl.BlockSpec((tm, da), lambda i, p: (i, 0)),
            pl.BlockSpec((None, dc, tn), lambda i, p: (layer, 0, ca(p))),
            pl.BlockSpec((None, da, tn), lambda i, p: (layer, 0, ca(p))),
            pl.BlockSpec((None, d, tn), lambda i, p: (layer, 0, ca(p))),
            pl.BlockSpec((None, d, tn), lambda i, p: (layer, 0, nc + ca(p))),
            pl.BlockSpec((None, 1, tn), lambda i, p: (layer, 0, ca(p))),
            pl.BlockSpec((None, 1, tn), lambda i, p: (layer, 0, nc + ca(p))),
            pl.BlockSpec((None, d, tn), lambda i, p: (layer, 0, cb_(p))),
            pl.BlockSpec((tm, tn), lambda i, p: (i, cb_(p))),
        ],
        out_specs=pl.BlockSpec((tm, tn), lambda i, p: (i, cb_(p))),
        out_shape=jax.ShapeDtypeStruct((n, d), F32),
        scratch_shapes=[pltpu.VMEM((nc, tm, tn), BF16)],
        compiler_params=_cparams(2),
        name="merge",
    )(h, cb, ab, wco, wo, wg, wg, bg, bg, wout, x)


def _ffn_kernel(x_ref, g_ref, wg_ref, wu_ref, wdn_ref, gf_ref, o_ref, h_ref, *, nf, final_norm):
    j = pl.program_id(1)

    @pl.when(j == 0)
    def _():
        x = x_ref[...]
        h_ref[...] = _rms_rows(x, g_ref[...]).astype(BF16)
        o_ref[...] = x

    h = h_ref[...]
    gate = jnp.dot(h, wg_ref[...], preferred_element_type=F32)
    up = jnp.dot(h, wu_ref[...], preferred_element_type=F32)
    act = ((gate * _sigmoid(gate)) * up).astype(BF16)
    o_ref[...] += jnp.dot(act, wdn_ref[...], preferred_element_type=F32)

    if final_norm:
        @pl.when(j == nf - 1)
        def _():
            o_ref[...] = _rms_rows(o_ref[...], gf_ref[...])


def _ffn(x, g, wup, wdn, gf, layer, *, final_norm, tm_pref=512, tf=512):
    n, d = x.shape
    dff = wdn.shape[1]
    tm = _row_tile(n, tm_pref)
    assert dff % tf == 0
    nf = dff // tf
    kern = functools.partial(_ffn_kernel, nf=nf, final_norm=final_norm)
    return pl.pallas_call(
        kern,
        grid=(n // tm, nf),
        in_specs=[
            pl.BlockSpec((tm, d), lambda i, j: (i, 0)),
            pl.BlockSpec((None, 1, d), lambda i, j: (layer, 0, 0)),
            pl.BlockSpec((None, d, tf), lambda i, j: (layer, 0, j)),
            pl.BlockSpec((None, d, tf), lambda i, j: (layer, 0, nf + j)),
            pl.BlockSpec((None, tf, d), lambda i, j: (layer, j, 0)),
            pl.BlockSpec((1, d), lambda i, j: (0, 0)),
        ],
        out_specs=pl.BlockSpec((tm, d), lambda i, j: (i, 0)),
        out_shape=jax.ShapeDtypeStruct((n, d), F32),
        scratch_shapes=[pltpu.VMEM((tm, d), BF16)],
        compiler_params=_cparams(2),
        name="ffn",
    )(x, g, wup, wup, wdn, gf)


def kernel(x_prompt, x_sample, cache_k, cache_v, state_conv, norm_mix, w_in, conv_w, conv_b, ln_g, ln_b,
           w_conv_out, rel_bias, w_o, w_gate, b_gate, w_out, norm_ffn, w_up, w_down, norm_final):
    depth = w_in.shape[0]
    bp, tp, d = x_prompt.shape
    bs, ts, _ = x_sample.shape
    dc = conv_w.shape[2]
    da = w_o.shape[1]
    dff = w_down.shape[1]
    keep_s = cache_k.shape[2]
    nh = da // HEAD_DIM
    assert ts == CHUNK and tp % CHUNK == 0
    tn_in = 512 if dc % 512 == 0 and da % 512 == 0 else 256
    tf = 512 if dff % 512 == 0 else 256

    w_in_b, w_up_b, w_down_b, w_co_b, w_o_b, w_gate_b, w_out_b = (
        w.astype(BF16) for w in (w_in, w_up, w_down, w_conv_out, w_o, w_gate, w_out))
    row3 = lambda a: a.reshape(depth, 1, a.shape[-1])
    g_mix, g_ffn, cb3, lg3, lb3, bg3 = map(row3, (norm_mix, norm_ffn, conv_b, ln_g, ln_b, b_gate))
    g_fin = norm_final.reshape(1, d)
    bias = _band_bias(rel_bias)
    ck = cache_k.reshape(depth, bs, keep_s, da)
    cv = cache_v.reshape(depth, bs, keep_s, da)
    pad_front = CONV_HALO - (CONV_WIDTH - 1)
    pre_p = jnp.zeros((bp, CONV_HALO, dc), F32)

    xp = x_prompt.reshape(bp * tp, d)
    xs = x_sample.reshape(bs * ts, d)
    keep_p = min(BAND_PAST, tp)
    kp, vp, cp, ksl, vsl, csl = [], [], [], [], [], []
    for l in range(depth):
        merge_w = (w_co_b, w_o_b, w_gate_b, bg3, w_out_b, l)
        last = l == depth - 1
        h, u, q, k, v, kt, vt = _inproj(xp, g_mix, w_in_b, l, seq_len=tp, d_conv=dc, d_attn=da, tn=tn_in)
        att = _attn_prompt(q, k, v, bias, l, seq_len=tp)
        cact = _conv_module(u, pre_p, conv_w, cb3, lg3, lb3, l, seq_len=tp)
        xp = _merge(xp, h, cact, att, *merge_w)
        xp = _ffn(xp, g_ffn, w_up_b, w_down_b, g_fin, l, final_norm=last, tf=tf)
        kp.append(kt.reshape(bp, keep_p, nh, HEAD_DIM))
        vp.append(vt.reshape(bp, keep_p, nh, HEAD_DIM))
        cp.append(u.reshape(bp, tp, dc)[:, tp - (CONV_WIDTH - 1):])
        h, u, q, k, v, kt, vt = _inproj(xs, g_mix, w_in_b, l, seq_len=ts, d_conv=dc, d_attn=da, tn=tn_in)
        att, nk, nv = _attn_sample(q, k, v, kt, vt, ck, cv, bias, l)
        pre_s = jnp.pad(state_conv[l], ((0, 0), (pad_front, 0), (0, 0)))
        cact = _conv_module(u, pre_s, conv_w, cb3, lg3, lb3, l, seq_len=ts)
        xs = _merge(xs, h, cact, att, *merge_w)
        xs = _ffn(xs, g_ffn, w_up_b, w_down_b, g_fin, l, final_norm=last, tf=tf)
        ksl.append(nk.reshape(bs, keep_s, nh, HEAD_DIM))
        vsl.append(nv.reshape(bs, keep_s, nh, HEAD_DIM))
        csl.append(u.reshape(bs, ts, dc)[:, ts - (CONV_WIDTH - 1):])
    return (xp.reshape(bp, tp, d), xs.reshape(bs, ts, d), jnp.stack(kp), jnp.stack(vp), jnp.stack(cp),
            jnp.stack(ksl), jnp.stack(vsl), jnp.stack(csl))
```

```python
import functools

import jax
import jax.numpy as jnp
from jax import lax
from jax.experimental import pallas as pl
from jax.experimental.pallas import tpu as pltpu

F32 = jnp.float32
BF16 = jnp.bfloat16

CHUNK = 64
PAST_CHUNKS = 8
BAND_PAST = PAST_CHUNKS * CHUNK
BAND = BAND_PAST + CHUNK
BAND_PAD = BAND + CHUNK
HEAD_DIM = 64
CONV_WIDTH = 31
CONV_HALO = 32
CONV_ROWS = 32
REL_CLIP = 128
EPS = 1e-6
NEG = -1e30

LANES = 128
SUBLANES = 8
PAIR = 2 * HEAD_DIM
VMEM_LIMIT_BYTES = 56 * 1024 * 1024
N_QKV_STEPS = 4


def _cparams(ndims):
    return pltpu.CompilerParams(dimension_semantics=("arbitrary",) * ndims,
                                vmem_limit_bytes=VMEM_LIMIT_BYTES)


def _sigmoid(x):
    return 1.0 / (1.0 + jnp.exp(-x))


def _rms_rows(x, g):
    ms = jnp.mean(x * x, axis=-1, keepdims=True)
    return (x * lax.rsqrt(ms + EPS)) * g


def _row_tile(n, pref):
    t = min(n, pref)
    assert n % t == 0, (n, t)
    return t


def _conv_taps(ext, cw_ref, cb_ref, y_ref, rows):
    c = ext.shape[1]
    erows = rows + CONV_HALO
    first = CONV_HALO - (CONV_WIDTH - 1)
    for s in range(SUBLANES):
        shifted = ext if s == 0 else pltpu.roll(ext, erows - s, 0)
        taps = [(q * SUBLANES, q * SUBLANES + s - first) for q in range(CONV_HALO // SUBLANES + 1)
                if 0 <= q * SUBLANES + s - first < CONV_WIDTH]
        for r in range(0, rows, CONV_ROWS):
            acc = jnp.broadcast_to(cb_ref[...], (CONV_ROWS, c)) if s == 0 else y_ref[r:r + CONV_ROWS, :]
            for off, w in taps:
                acc = acc + shifted[r + off:r + off + CONV_ROWS, :] * cw_ref[w:w + 1, :]
            y_ref[r:r + CONV_ROWS, :] = acc


def _ln_swish(y, lg, lb):
    mu = jnp.mean(y, axis=-1, keepdims=True)
    d = y - mu
    var = jnp.mean(d * d, axis=-1, keepdims=True)
    z = (d * lax.rsqrt(var + EPS)) * lg + lb
    return z * _sigmoid(z)


def _inproj_kernel(*refs, nu, nq, tm, tu, tail_rows, tail_pred, bps, fuse_conv):
    if fuse_conv:
        (x_ref, g_ref, wa_ref, wb_ref, wq_ref, cw_ref, cb_ref, lg_ref, lb_ref,
         h_ref, c_ref, qkv_ref, tail_ref, st_ref, uext_ref, y_ref) = refs
    else:
        x_ref, g_ref, wa_ref, wb_ref, wq_ref, h_ref, u_ref, qkv_ref, tail_ref, uext_ref = refs
    i = pl.program_id(0)
    j = pl.program_id(1)
    ib = i % bps

    @pl.when(j == 0)
    def _():
        h_ref[...] = _rms_rows(x_ref[...], g_ref[...]).astype(BF16)
        if fuse_conv:
            @pl.when(ib == 0)
            def _():
                uext_ref[0:CONV_HALO, :] = jnp.zeros((CONV_HALO, uext_ref.shape[1]), F32)

            @pl.when(ib != 0)
            def _():
                uext_ref[0:CONV_HALO, :] = uext_ref[tm:tm + CONV_HALO, :]

    for jj in range(nu):
        @pl.when(j == jj)
        def _():
            h = h_ref[...]
            a = jnp.dot(h, wa_ref[...], preferred_element_type=F32)
            b = jnp.dot(h, wb_ref[...], preferred_element_type=F32)
            uext_ref[CONV_HALO:, jj * tu:(jj + 1) * tu] = a * _sigmoid(b)

    is_tail = tail_pred(i)

    @pl.when(j >= nu)
    def _():
        p = jnp.dot(h_ref[...], wq_ref[...], preferred_element_type=F32)
        qkv_ref[...] = p.astype(BF16)
        if fuse_conv:
            rows = tm // nq
            r0 = pl.multiple_of((j - nu) * rows, rows)
            _conv_taps(uext_ref[pl.ds(r0, rows + CONV_HALO), :], cw_ref, cb_ref, y_ref, rows)
            c_ref[pl.ds(r0, rows), :] = _ln_swish(y_ref[...], lg_ref[...], lb_ref[...]).astype(BF16)

        @pl.when(is_tail)
        def _():
            tail_ref[...] = p[tm - tail_rows:, :]

        if fuse_conv:
            @pl.when((j == nu) & (ib == bps - 1))
            def _():
                st_ref[...] = uext_ref[tm:tm + CONV_HALO, :]
        else:
            @pl.when(j == nu)
            def _():
                u_ref[...] = uext_ref[CONV_HALO:, :]


def _inproj(x, g, w_ab, w_qkv, layer, *, seq_len, conv=None, tm_pref=512):
    n, d = x.shape
    dc, dqkv = w_ab.shape[2] // 2, w_qkv.shape[2]
    tm = _row_tile(n, tm_pref)
    nu, nq = 2, N_QKV_STEPS
    tu, tq = dc // nu, dqkv // nq
    assert tu % LANES == 0 and tq % LANES == 0
    ni, nj = n // tm, nu + nq
    keep = min(BAND_PAST, seq_len)
    fuse_conv = conv is not None
    if seq_len % tm == 0:
        bps = seq_len // tm
        assert tm % keep == 0 or keep % tm == 0
        tail_rows = min(tm, keep)
        ntb = keep // tail_rows
        n_tail = (n // seq_len) * keep

        def tail_blk(i):
            return (i // bps) * ntb + jnp.clip(i % bps - (bps - ntb), 0, ntb - 1)

        def tail_pred(i):
            return (i % bps) >= (bps - ntb)
    else:
        assert tm % seq_len == 0 and keep == seq_len and not fuse_conv
        bps, tail_rows, n_tail = 1, tm, n

        def tail_blk(i):
            return i

        def tail_pred(i):
            return i >= 0

    kern = functools.partial(_inproj_kernel, nu=nu, nq=nq, tm=tm, tu=tu, tail_rows=tail_rows,
                             tail_pred=tail_pred, bps=bps, fuse_conv=fuse_conv)
    ucol = lambda j: jnp.minimum(j, nu - 1)
    qcol = lambda j: jnp.clip(j - nu, 0, nq - 1)
    vec = pl.BlockSpec((None, 1, dc), lambda i, j: (layer, 0, 0))
    in_specs = [
        pl.BlockSpec((tm, d), lambda i, j: (i, 0)),
        pl.BlockSpec((None, 1, d), lambda i, j: (layer, 0, 0)),
        pl.BlockSpec((None, d, tu), lambda i, j: (layer, 0, ucol(j))),
        pl.BlockSpec((None, d, tu), lambda i, j: (layer, 0, nu + ucol(j))),
        pl.BlockSpec((None, d, tq), lambda i, j: (layer, 0, qcol(j))),
    ]
    args = [x, g, w_ab, w_ab, w_qkv]
    out_specs = [
        pl.BlockSpec((tm, d), lambda i, j: (i, 0)),
        pl.BlockSpec((tm, dc), lambda i, j: (i, 0)),
        pl.BlockSpec((tm, tq), lambda i, j: (i, qcol(j))),
        pl.BlockSpec((tail_rows, tq), lambda i, j: (tail_blk(i), jnp.where(tail_pred(i), qcol(j), 0))),
    ]
    out_shape = [
        jax.ShapeDtypeStruct((n, d), BF16),
        jax.ShapeDtypeStruct((n, dc), BF16 if fuse_conv else F32),
        jax.ShapeDtypeStruct((n, dqkv), BF16),
        jax.ShapeDtypeStruct((n_tail, dqkv), F32),
    ]
    scratch = [pltpu.VMEM((tm + CONV_HALO, dc), F32)]
    if fuse_conv:
        in_specs += [pl.BlockSpec((None, CONV_WIDTH, dc), lambda i, j: (layer, 0, 0)), vec, vec, vec]
        args += list(conv)
        out_specs.append(pl.BlockSpec((None, CONV_HALO, dc), lambda i, j: (i // bps, 0, 0)))
        out_shape.append(jax.ShapeDtypeStruct((n // seq_len, CONV_HALO, dc), F32))
        scratch.append(pltpu.VMEM((tm // nq, dc), F32))
    return pl.pallas_call(
        kern,
        grid=(ni, nj),
        in_specs=in_specs,
        out_specs=out_specs,
        out_shape=out_shape,
        scratch_shapes=scratch,
        compiler_params=_cparams(2),
        name="inproj_conv" if fuse_conv else "inproj",
    )(*args)


def _conv_kernel(u_ref, halo_ref, pre_ref, cw_ref, cb_ref, lg_ref, lb_ref, o_ref, ext_ref, y_ref, *, tt):
    t = pl.program_id(1)

    @pl.when(t == 0)
    def _():
        ext_ref[0:CONV_HALO, :] = pre_ref[...]

    @pl.when(t > 0)
    def _():
        ext_ref[0:CONV_HALO, :] = halo_ref[...]

    ext_ref[CONV_HALO:, :] = u_ref[...]
    _conv_taps(ext_ref[...], cw_ref, cb_ref, y_ref, tt)
    o_ref[...] = _ln_swish(y_ref[...], lg_ref[...], lb_ref[...]).astype(o_ref.dtype)


def _conv_module(u, prefix, conv_w, conv_b, ln_g, ln_b, layer, *, seq_len, tt_pref=256):
    n, c = u.shape
    b = n // seq_len
    tt = _row_tile(seq_len, tt_pref)
    assert tt % CONV_ROWS == 0
    nt = seq_len // tt
    hb = tt // CONV_HALO
    kern = functools.partial(_conv_kernel, tt=tt)
    vec = pl.BlockSpec((None, 1, c), lambda i, t: (layer, 0, 0))
    return pl.pallas_call(
        kern,
        grid=(b, nt),
        in_specs=[
            pl.BlockSpec((tt, c), lambda i, t: (i * nt + t, 0)),
            pl.BlockSpec((CONV_HALO, c), lambda i, t: (jnp.maximum((i * nt + t) * hb - 1, 0), 0)),
            pl.BlockSpec((None, CONV_HALO, c), lambda i, t: (i, 0, 0)),
            pl.BlockSpec((None, CONV_WIDTH, c), lambda i, t: (layer, 0, 0)),
            vec, vec, vec,
        ],
        out_specs=pl.BlockSpec((tt, c), lambda i, t: (i * nt + t, 0)),
        out_shape=jax.ShapeDtypeStruct((n, c), BF16),
        scratch_shapes=[pltpu.VMEM((tt + CONV_HALO, c), F32), pltpu.VMEM((tt, c), F32)],
        compiler_params=_cparams(2),
        name="conv_module",
    )(u, u, prefix, conv_w, conv_b, ln_g, ln_b)


def _pair_scores(q2, kb):
    lane = lax.broadcasted_iota(jnp.int32, (CHUNK, PAIR), 1)
    lo = lane < HEAD_DIM
    qs = q2 * jnp.asarray(HEAD_DIM ** -0.5, q2.dtype)
    zero = jnp.zeros_like(qs)
    qs = jnp.concatenate([jnp.where(lo, qs, zero), jnp.where(lo, zero, qs)], axis=0)
    return lax.dot_general(qs, kb, (((1,), (1,)), ((), ())), preferred_element_type=F32)


def _pair_softmax_pv(s, bias, vb, n_invalid):
    s = s + bias
    if n_invalid is None:
        first = s[:, :LANES]
        col = lax.broadcasted_iota(jnp.int32, first.shape, 1)
        s = jnp.concatenate([jnp.where(col >= CHUNK, first, NEG), s[:, LANES:]], axis=1)
    else:
        col = lax.broadcasted_iota(jnp.int32, s.shape, 1)
        s = jnp.where(col >= n_invalid, s, NEG)
    m = jnp.max(s, axis=-1, keepdims=True)
    e = jnp.exp(s - m)
    l = jnp.sum(e, axis=-1, keepdims=True)
    r = jnp.dot(e.astype(BF16), vb, preferred_element_type=F32)
    r = r * (1.0 / l)
    lane = lax.broadcasted_iota(jnp.int32, (CHUNK, PAIR), 1)
    return jnp.where(lane < HEAD_DIM, r[:CHUNK], r[CHUNK:])


def _attn_prompt_kernel(q_ref, k_ref, v_ref, bias_ref, o_ref, kp_ref, vp_ref, s_ref, *, seq_len, npairs):
    pad = BAND_PAD - CHUNK
    kp_ref[0:pad, :] = jnp.zeros((pad, kp_ref.shape[1]), BF16)
    vp_ref[0:pad, :] = jnp.zeros((pad, vp_ref.shape[1]), BF16)
    kp_ref[pad:, :] = k_ref[...]
    vp_ref[pad:, :] = v_ref[...]
    n_chunks = seq_len // CHUNK
    n_masked = min(PAST_CHUNKS, n_chunks)
    assert n_chunks % 2 == 0 and n_masked % 2 == 0

    def scores(c, slot):
        r0 = pl.multiple_of(c * CHUNK, CHUNK)
        for hp in range(npairs):
            cols = slice(hp * PAIR, (hp + 1) * PAIR)
            s_ref[slot, hp] = _pair_scores(q_ref[pl.ds(r0, CHUNK), cols], kp_ref[pl.ds(r0, BAND_PAD), cols])

    def finish(c, slot, masked):
        r0 = pl.multiple_of(c * CHUNK, CHUNK)
        n_invalid = (PAST_CHUNKS + 1 - c) * CHUNK if masked else None
        for hp in range(npairs):
            cols = slice(hp * PAIR, (hp + 1) * PAIR)
            o2 = _pair_softmax_pv(s_ref[slot, hp], bias_ref[hp], vp_ref[pl.ds(r0, BAND_PAD), cols], n_invalid)
            o_ref[pl.ds(r0, CHUNK), cols] = o2.astype(o_ref.dtype)

    def two_chunks(cc, masked):
        c = 2 * cc
        scores(c + 1, 1)
        finish(c, 0, masked)
        scores(jnp.minimum(c + 2, n_chunks - 1), 0)
        finish(c + 1, 1, masked)

    def masked_body(cc, carry):
        two_chunks(cc, True)
        return carry

    def plain_body(cc, carry):
        two_chunks(cc, False)
        return carry

    scores(0, 0)
    lax.fori_loop(0, n_masked // 2, masked_body, 0)
    lax.fori_loop(n_masked // 2, n_chunks // 2, plain_body, 0)


def _attn_prompt(qkv, bias, layer, *, seq_len, group_pairs=4):
    n, da = qkv.shape[0], qkv.shape[1] // 3
    b = n // seq_len
    gp = min(group_pairs, da // PAIR)
    gw = gp * PAIR
    assert da % gw == 0
    ng = da // gw
    kern = functools.partial(_attn_prompt_kernel, seq_len=seq_len, npairs=gp)
    part = lambda k: pl.BlockSpec((seq_len, gw), lambda i, g: (i, k * ng + g))
    return pl.pallas_call(
        kern,
        grid=(b, ng),
        in_specs=[part(0), part(1), part(2),
                  pl.BlockSpec((None, gp, PAIR, BAND_PAD), lambda i, g: (layer, g, 0, 0))],
        out_specs=part(0),
        out_shape=jax.ShapeDtypeStruct((n, da), BF16),
        scratch_shapes=[pltpu.VMEM((seq_len + BAND_PAD - CHUNK, gw), BF16),
                        pltpu.VMEM((seq_len + BAND_PAD - CHUNK, gw), BF16),
                        pltpu.VMEM((2, gp, PAIR, BAND_PAD), F32)],
        compiler_params=_cparams(2),
        name="attn_prompt",
    )(qkv, qkv, qkv, bias)


def _attn_sample_kernel(q_ref, k_ref, v_ref, kt_ref, vt_ref, ck_ref, cv_ref, bias_ref,
                        o_ref, nk_ref, nv_ref, kb_ref, vb_ref, *, npairs):
    keep = ck_ref.shape[0]
    da = ck_ref.shape[1]
    kb_ref[0:CHUNK, :] = jnp.zeros((CHUNK, da), BF16)
    vb_ref[0:CHUNK, :] = jnp.zeros((CHUNK, da), BF16)
    kb_ref[CHUNK:CHUNK + keep, :] = ck_ref[...].astype(BF16)
    vb_ref[CHUNK:CHUNK + keep, :] = cv_ref[...].astype(BF16)
    kb_ref[CHUNK + keep:, :] = k_ref[...]
    vb_ref[CHUNK + keep:, :] = v_ref[...]
    nk_ref[0:keep - CHUNK, :] = ck_ref[CHUNK:, :]
    nv_ref[0:keep - CHUNK, :] = cv_ref[CHUNK:, :]
    nk_ref[keep - CHUNK:, :] = kt_ref[...]
    nv_ref[keep - CHUNK:, :] = vt_ref[...]
    cols = [slice(hp * PAIR, (hp + 1) * PAIR) for hp in range(npairs)]
    ss = [_pair_scores(q_ref[:, c], kb_ref[:, c]) for c in cols]
    for hp, c in enumerate(cols):
        o_ref[:, c] = _pair_softmax_pv(ss[hp], bias_ref[hp], vb_ref[:, c], None).astype(o_ref.dtype)


def _attn_sample(qkv, qkv_f32, cache_k, cache_v, bias, layer):
    n, da = qkv.shape[0], qkv.shape[1] // 3
    b = n // CHUNK
    keep = cache_k.shape[2]
    assert keep == BAND_PAST
    npairs = da // PAIR
    kern = functools.partial(_attn_sample_kernel, npairs=npairs)
    part = lambda k: pl.BlockSpec((CHUNK, da), lambda i: (i, k))
    cache = pl.BlockSpec((None, None, keep, da), lambda i: (layer, i, 0, 0))
    newc = pl.BlockSpec((None, keep, da), lambda i: (i, 0, 0))
    return pl.pallas_call(
        kern,
        grid=(b,),
        in_specs=[part(0), part(1), part(2), part(1), part(2), cache, cache,
                  pl.BlockSpec((None, npairs, PAIR, BAND_PAD), lambda i: (layer, 0, 0, 0))],
        out_specs=[part(0), newc, newc],
        out_shape=[jax.ShapeDtypeStruct((n, da), BF16),
                   jax.ShapeDtypeStruct((b, keep, da), F32),
                   jax.ShapeDtypeStruct((b, keep, da), F32)],
        scratch_shapes=[pltpu.VMEM((BAND_PAD, da), BF16), pltpu.VMEM((BAND_PAD, da), BF16)],
        compiler_params=_cparams(1),
        name="attn_sample",
    )(qkv, qkv, qkv, qkv_f32, qkv_f32, cache_k, cache_v, bias)


def _band_bias(rel_bias):
    depth, h, _ = rel_bias.shape
    n_far = BAND + CHUNK - 1 - REL_CLIP + 1
    n_near = BAND_PAD + CHUNK - 1 - n_far
    far = jnp.broadcast_to(rel_bias[:, :, 2 * REL_CLIP:], (depth, h, n_far))
    near = lax.rev(rel_bias[:, :, 2 * REL_CLIP - n_near:2 * REL_CLIP], (2,))
    g = jnp.concatenate([far, near], axis=2)
    rows = [g[:, :, CHUNK - 1 - i:CHUNK - 1 - i + BAND_PAD] for i in range(CHUNK)]
    return jnp.stack(rows, axis=2).reshape(depth, h // 2, PAIR, BAND_PAD).astype(F32)


def _merge_kernel(h_ref, c_ref, a_ref, wco_ref, wo_ref, wg1_ref, wg2_ref, b1_ref, b2_ref,
                  wout_ref, xt_ref, o_ref, m_ref, *, nc, tn):
    p = pl.program_id(1)

    @pl.when(p < nc)
    def _():
        h = h_ref[...]
        gc = _sigmoid(jnp.dot(h, wg1_ref[...], preferred_element_type=F32) + b1_ref[...])
        yc = jnp.dot(c_ref[...], wco_ref[...], preferred_element_type=F32)
        mc = gc * yc
        ga = _sigmoid(jnp.dot(h, wg2_ref[...], preferred_element_type=F32) + b2_ref[...])
        ya = jnp.dot(a_ref[...], wo_ref[...], preferred_element_type=F32)
        m_ref[p] = (mc + ga * ya).astype(BF16)

    @pl.when(p >= nc)
    def _():
        acc = xt_ref[...]
        for cc in range(nc):
            acc = acc + jnp.dot(m_ref[cc], wout_ref[cc * tn:(cc + 1) * tn, :], preferred_element_type=F32)
        o_ref[...] = acc


def _merge(x, h, cb, ab, wco, wo, wg, bg, wout, layer, *, tm_pref=1024, tn=512):
    n, d = x.shape
    dc, da = cb.shape[1], ab.shape[1]
    tm = _row_tile(n, tm_pref)
    assert d % tn == 0
    nc = d // tn
    kern = functools.partial(_merge_kernel, nc=nc, tn=tn)
    ca = lambda p: jnp.minimum(p, nc - 1)
    cb_ = lambda p: jnp.maximum(p - nc, 0)
    return pl.pallas_call(
        kern,
        grid=(n // tm, 2 * nc),
        in_specs=[
            pl.BlockSpec((tm, d), lambda i, p: (i, 0)),
            pl.BlockSpec((tm, dc), lambda i, p: (i, 0)),
            pl.BlockSpec((tm, da), lambda i, p: (i, 0)),
            pl.BlockSpec((None, dc, tn), lambda i, p: (layer, 0, ca(p))),
            pl.BlockSpec((None, da, tn), lambda i, p: (layer, 0, ca(p))),
            pl.BlockSpec((None, d, tn), lambda i, p: (layer, 0, ca(p))),
            pl.BlockSpec((None, d, tn), lambda i, p: (layer, 0, nc + ca(p))),
            pl.BlockSpec((None, 1, tn), lambda i, p: (layer, 0, ca(p))),
            pl.BlockSpec((None, 1, tn), lambda i, p: (layer, 0, nc + ca(p))),
            pl.BlockSpec((None, d, tn), lambda i, p: (layer, 0, cb_(p))),
            pl.BlockSpec((tm, tn), lambda i, p: (i, cb_(p))),
        ],
        out_specs=pl.BlockSpec((tm, tn), lambda i, p: (i, cb_(p))),
        out_shape=jax.ShapeDtypeStruct((n, d), F32),
        scratch_shapes=[pltpu.VMEM((nc, tm, tn), BF16)],
        compiler_params=_cparams(2),
        name="merge",
    )(h, cb, ab, wco, wo, wg, wg, bg, bg, wout, x)


def _ffn_kernel(x_ref, g_ref, wg_ref, wu_ref, wdn_ref, gf_ref, o_ref, h_ref, *, nf, final_norm):
    j = pl.program_id(1)

    @pl.when(j == 0)
    def _():
        x = x_ref[...]
        h_ref[...] = _rms_rows(x, g_ref[...]).astype(BF16)
        o_ref[...] = x

    h = h_ref[...]
    gate = jnp.dot(h, wg_ref[...], preferred_element_type=F32)
    up = jnp.dot(h, wu_ref[...], preferred_element_type=F32)
    act = ((gate * _sigmoid(gate)) * up).astype(BF16)
    o_ref[...] += jnp.dot(act, wdn_ref[...], preferred_element_type=F32)

    if final_norm:
        @pl.when(j == nf - 1)
        def _():
            o_ref[...] = _rms_rows(o_ref[...], gf_ref[...])


def _ffn(x, g, wup, wdn, gf, layer, *, final_norm, tm_pref=512, tf=512):
    n, d = x.shape
    dff = wdn.shape[1]
    tm = _row_tile(n, tm_pref)
    assert dff % tf == 0
    nf = dff // tf
    kern = functools.partial(_ffn_kernel, nf=nf, final_norm=final_norm)
    return pl.pallas_call(
        kern,
        grid=(n // tm, nf),
        in_specs=[
            pl.BlockSpec((tm, d), lambda i, j: (i, 0)),
            pl.BlockSpec((None, 1, d), lambda i, j: (layer, 0, 0)),
            pl.BlockSpec((None, d, tf), lambda i, j: (layer, 0, j)),
            pl.BlockSpec((None, d, tf), lambda i, j: (layer, 0, nf + j)),
            pl.BlockSpec((None, tf, d), lambda i, j: (layer, j, 0)),
            pl.BlockSpec((1, d), lambda i, j: (0, 0)),
        ],
        out_specs=pl.BlockSpec((tm, d), lambda i, j: (i, 0)),
        out_shape=jax.ShapeDtypeStruct((n, d), F32),
        scratch_shapes=[pltpu.VMEM((tm, d), BF16)],
        compiler_params=_cparams(2),
        name="ffn",
    )(x, g, wup, wup, wdn, gf)


def kernel(x_prompt, x_sample, cache_k, cache_v, state_conv, norm_mix, w_in, conv_w, conv_b, ln_g, ln_b,
           w_conv_out, rel_bias, w_o, w_gate, b_gate, w_out, norm_ffn, w_up, w_down, norm_final):
    depth = w_in.shape[0]
    bp, tp, d = x_prompt.shape
    bs, ts, _ = x_sample.shape
    dc = conv_w.shape[2]
    da = w_o.shape[1]
    dff = w_down.shape[1]
    keep_s = cache_k.shape[2]
    nh = da // HEAD_DIM
    n_state = CONV_WIDTH - 1
    assert ts == CHUNK and tp % CHUNK == 0
    tf = 512 if dff % 512 == 0 else 256

    w_ab_b, w_qkv_b = w_in[:, :, :2 * dc].astype(BF16), w_in[:, :, 2 * dc:].astype(BF16)
    w_up_b, w_down_b, w_co_b, w_o_b, w_gate_b, w_out_b = (
        w.astype(BF16) for w in (w_up, w_down, w_conv_out, w_o, w_gate, w_out))
    row3 = lambda a: a.reshape(depth, 1, a.shape[-1])
    g_mix, g_ffn, cb3, lg3, lb3, bg3 = map(row3, (norm_mix, norm_ffn, conv_b, ln_g, ln_b, b_gate))
    g_fin = norm_final.reshape(1, d)
    bias = _band_bias(rel_bias)
    ck = cache_k.reshape(depth, bs, keep_s, da)
    cv = cache_v.reshape(depth, bs, keep_s, da)
    conv_p = (conv_w, cb3, lg3, lb3)

    xp = x_prompt.reshape(bp * tp, d)
    xs = x_sample.reshape(bs * ts, d)
    keep_p = min(BAND_PAST, tp)
    heads = lambda a, b, keep: a.reshape(b, keep, nh, HEAD_DIM)
    kp, vp, cp, ksl, vsl, csl = [], [], [], [], [], []
    for l in range(depth):
        merge_w = (w_co_b, w_o_b, w_gate_b, bg3, w_out_b, l)
        last = l == depth - 1
        h, cact, qkv, tail, ust = _inproj(xp, g_mix, w_ab_b, w_qkv_b, l, seq_len=tp, conv=conv_p)
        att = _attn_prompt(qkv, bias, l, seq_len=tp)
        xp = _merge(xp, h, cact, att, *merge_w)
        xp = _ffn(xp, g_ffn, w_up_b, w_down_b, g_fin, l, final_norm=last, tf=tf)
        kp.append(heads(tail[:, da:2 * da], bp, keep_p))
        vp.append(heads(tail[:, 2 * da:], bp, keep_p))
        cp.append(ust[:, CONV_HALO - n_state:])
        h, u, qkv, tail = _inproj(xs, g_mix, w_ab_b, w_qkv_b, l, seq_len=ts)
        att, nk, nv = _attn_sample(qkv, tail, ck, cv, bias, l)
        pre_s = jnp.pad(state_conv[l], ((0, 0), (CONV_HALO - n_state, 0), (0, 0)))
        cact = _conv_module(u, pre_s, *conv_p, l, seq_len=ts)
        xs = _merge(xs, h, cact, att, *merge_w)
        xs = _ffn(xs, g_ffn, w_up_b, w_down_b, g_fin, l, final_norm=last, tf=tf)
        ksl.append(heads(nk, bs, keep_s))
        vsl.append(heads(nv, bs, keep_s))
        csl.append(u.reshape(bs, ts, dc)[:, ts - n_state:])
    return (xp.reshape(bp, tp, d), xs.reshape(bs, ts, d), jnp.stack(kp), jnp.stack(vp), jnp.stack(cp),
            jnp.stack(ksl), jnp.stack(vsl), jnp.stack(csl))
```

```python
import functools

import jax
import jax.numpy as jnp
from jax import lax
from jax.experimental import pallas as pl
from jax.experimental.pallas import tpu as pltpu

F32 = jnp.float32
BF16 = jnp.bfloat16

CHUNK = 64
PAST_CHUNKS = 8
BAND_PAST = PAST_CHUNKS * CHUNK
BAND = BAND_PAST + CHUNK
BAND_PAD = BAND + CHUNK
HEAD_DIM = 64
CONV_WIDTH = 31
CONV_HALO = 32
CONV_ROWS = 128
REL_CLIP = 128
EPS = 1e-6
NEG = -1e30
LOG2E = 1.4426950408889634
QK_SCALE = HEAD_DIM ** -0.5 * LOG2E

LANES = 128
SUBLANES = 8
PAIR = 2 * HEAD_DIM
VMEM_LIMIT_BYTES = 56 * 1024 * 1024
N_QKV_STEPS = 4


def _cparams(ndims):
    return pltpu.CompilerParams(dimension_semantics=("arbitrary",) * ndims,
                                vmem_limit_bytes=VMEM_LIMIT_BYTES)


def _sigmoid(x):
    return 1.0 / (1.0 + jnp.exp(-x))


def _rms_rows(x, g):
    ms = jnp.mean(x * x, axis=-1, keepdims=True)
    return (x * lax.rsqrt(ms + EPS)) * g


def _row_tile(n, pref):
    t = min(n, pref)
    assert n % t == 0, (n, t)
    return t


def _strips(c):
    assert c % LANES == 0
    return [slice(s * LANES, (s + 1) * LANES) for s in range(c // LANES)]


def _conv_taps(uext_ref, base, cw_ref, cb_ref, y_ref, rows):
    first = CONV_HALO - (CONV_WIDTH - 1)
    group = min(rows, CONV_ROWS)
    assert rows % group == 0 and group % SUBLANES == 0
    for s, lanes in enumerate(_strips(y_ref.shape[1])):
        for r in range(0, rows, group):
            accs = [cb_ref[:, lanes]] * (group // SUBLANES)
            for w in range(CONV_WIDTH):
                wt = cw_ref[w, :, lanes]
                accs = [a + uext_ref[s, pl.ds(base + r + t * SUBLANES + first + w, SUBLANES), :] * wt
                        for t, a in enumerate(accs)]
            for t, a in enumerate(accs):
                y_ref[r + t * SUBLANES:r + (t + 1) * SUBLANES, lanes] = a


def _ln_swish(y, lg, lb):
    mu = jnp.mean(y, axis=-1, keepdims=True)
    d = y - mu
    var = jnp.mean(d * d, axis=-1, keepdims=True)
    z = (d * lax.rsqrt(var + EPS)) * lg + lb
    return z * _sigmoid(z)


def _inproj_kernel(*refs, nu, nq, tm, tu, tail_rows, tail_pred, bps, fuse_conv):
    if fuse_conv:
        (x_ref, g_ref, wa_ref, wb_ref, wq_ref, sc_ref, cw_ref, cb_ref, lg_ref, lb_ref,
         h_ref, c_ref, qkv_ref, tail_ref, st_ref, uext_ref, y_ref) = refs
    else:
        x_ref, g_ref, wa_ref, wb_ref, wq_ref, sc_ref, h_ref, u_ref, qkv_ref, tail_ref, uext_ref = refs
    i = pl.program_id(0)
    j = pl.program_id(1)
    ib = i % bps

    @pl.when(j == 0)
    def _():
        h_ref[...] = _rms_rows(x_ref[...], g_ref[...]).astype(BF16)
        if fuse_conv:
            @pl.when(ib == 0)
            def _():
                uext_ref[:, 0:CONV_HALO, :] = jnp.zeros((uext_ref.shape[0], CONV_HALO, LANES), F32)

            @pl.when(ib != 0)
            def _():
                uext_ref[:, 0:CONV_HALO, :] = uext_ref[:, tm:tm + CONV_HALO, :]

    for jj in range(nu):
        @pl.when(j == jj)
        def _():
            h = h_ref[...]
            a = jnp.dot(h, wa_ref[...], preferred_element_type=F32)
            b = jnp.dot(h, wb_ref[...], preferred_element_type=F32)
            u = a * _sigmoid(b)
            for s, lanes in enumerate(_strips(tu)):
                uext_ref[jj * (tu // LANES) + s, CONV_HALO:, :] = u[:, lanes]

    is_tail = tail_pred(i)

    @pl.when(j >= nu)
    def _():
        p = jnp.dot(h_ref[...], wq_ref[...], preferred_element_type=F32)
        qkv_ref[...] = (p * sc_ref[...]).astype(BF16)
        if fuse_conv:
            rows = tm // nq
            r0 = pl.multiple_of((j - nu) * rows, rows)
            _conv_taps(uext_ref, r0, cw_ref, cb_ref, y_ref, rows)
            c_ref[pl.ds(r0, rows), :] = _ln_swish(y_ref[...], lg_ref[...], lb_ref[...]).astype(BF16)

        @pl.when(is_tail)
        def _():
            tail_ref[...] = p[tm - tail_rows:, :]

        if fuse_conv:
            @pl.when((j == nu) & (ib == bps - 1))
            def _():
                for s, lanes in enumerate(_strips(st_ref.shape[1])):
                    st_ref[:, lanes] = uext_ref[s, tm:tm + CONV_HALO, :]
        else:
            @pl.when(j == nu)
            def _():
                for s, lanes in enumerate(_strips(u_ref.shape[1])):
                    u_ref[:, lanes] = uext_ref[s, CONV_HALO:, :]


def _inproj(x, g, w_ab, w_qkv, qkv_scale, layer, *, seq_len, conv=None, tm_pref=512):
    n, d = x.shape
    dc, dqkv = w_ab.shape[2] // 2, w_qkv.shape[2]
    tm = _row_tile(n, tm_pref)
    nu, nq = 2, N_QKV_STEPS
    tu, tq = dc // nu, dqkv // nq
    assert tu % LANES == 0 and tq % LANES == 0
    ni, nj = n // tm, nu + nq
    keep = min(BAND_PAST, seq_len)
    fuse_conv = conv is not None
    if seq_len % tm == 0:
        bps = seq_len // tm
        assert tm % keep == 0 or keep % tm == 0
        tail_rows = min(tm, keep)
        ntb = keep // tail_rows
        n_tail = (n // seq_len) * keep

        def tail_blk(i):
            return (i // bps) * ntb + jnp.clip(i % bps - (bps - ntb), 0, ntb - 1)

        def tail_pred(i):
            return (i % bps) >= (bps - ntb)
    else:
        assert tm % seq_len == 0 and keep == seq_len and not fuse_conv
        bps, tail_rows, n_tail = 1, tm, n

        def tail_blk(i):
            return i

        def tail_pred(i):
            return i >= 0

    kern = functools.partial(_inproj_kernel, nu=nu, nq=nq, tm=tm, tu=tu, tail_rows=tail_rows,
                             tail_pred=tail_pred, bps=bps, fuse_conv=fuse_conv)
    ucol = lambda j: jnp.minimum(j, nu - 1)
    qcol = lambda j: jnp.clip(j - nu, 0, nq - 1)
    vec = pl.BlockSpec((None, 1, dc), lambda i, j: (layer, 0, 0))
    in_specs = [
        pl.BlockSpec((tm, d), lambda i, j: (i, 0)),
        pl.BlockSpec((None, 1, d), lambda i, j: (layer, 0, 0)),
        pl.BlockSpec((None, d, tu), lambda i, j: (layer, 0, ucol(j))),
        pl.BlockSpec((None, d, tu), lambda i, j: (layer, 0, nu + ucol(j))),
        pl.BlockSpec((None, d, tq), lambda i, j: (layer, 0, qcol(j))),
        pl.BlockSpec((1, tq), lambda i, j: (0, qcol(j))),
    ]
    args = [x, g, w_ab, w_ab, w_qkv, qkv_scale]
    out_specs = [
        pl.BlockSpec((tm, d), lambda i, j: (i, 0)),
        pl.BlockSpec((tm, dc), lambda i, j: (i, 0)),
        pl.BlockSpec((tm, tq), lambda i, j: (i, qcol(j))),
        pl.BlockSpec((tail_rows, tq), lambda i, j: (tail_blk(i), jnp.where(tail_pred(i), qcol(j), 0))),
    ]
    out_shape = [
        jax.ShapeDtypeStruct((n, d), BF16),
        jax.ShapeDtypeStruct((n, dc), BF16 if fuse_conv else F32),
        jax.ShapeDtypeStruct((n, dqkv), BF16),
        jax.ShapeDtypeStruct((n_tail, dqkv), F32),
    ]
    scratch = [pltpu.VMEM((dc // LANES, tm + CONV_HALO, LANES), F32)]
    if fuse_conv:
        in_specs += [pl.BlockSpec((None, CONV_WIDTH, SUBLANES, dc), lambda i, j: (layer, 0, 0, 0)),
                     pl.BlockSpec((None, SUBLANES, dc), lambda i, j: (layer, 0, 0)), vec, vec]
        args += list(conv)
        out_specs.append(pl.BlockSpec((None, CONV_HALO, dc), lambda i, j: (i // bps, 0, 0)))
        out_shape.append(jax.ShapeDtypeStruct((n // seq_len, CONV_HALO, dc), F32))
        scratch.append(pltpu.VMEM((tm // nq, dc), F32))
    return pl.pallas_call(
        kern,
        grid=(ni, nj),
        in_specs=in_specs,
        out_specs=out_specs,
        out_shape=out_shape,
        scratch_shapes=scratch,
        compiler_params=_cparams(2),
        name="inproj_conv" if fuse_conv else "inproj",
    )(*args)


def _conv_kernel(u_ref, halo_ref, pre_ref, cw_ref, cb_ref, lg_ref, lb_ref, o_ref, ext_ref, y_ref, *, tt):
    t = pl.program_id(1)

    strips = _strips(u_ref.shape[1])

    @pl.when(t == 0)
    def _():
        for s, lanes in enumerate(strips):
            ext_ref[s, 0:CONV_HALO, :] = pre_ref[:, lanes]

    @pl.when(t > 0)
    def _():
        for s, lanes in enumerate(strips):
            ext_ref[s, 0:CONV_HALO, :] = halo_ref[:, lanes]

    for s, lanes in enumerate(strips):
        ext_ref[s, CONV_HALO:, :] = u_ref[:, lanes]
    _conv_taps(ext_ref, 0, cw_ref, cb_ref, y_ref, tt)
    o_ref[...] = _ln_swish(y_ref[...], lg_ref[...], lb_ref[...]).astype(o_ref.dtype)


def _conv_module(u, prefix, conv_w, conv_b, ln_g, ln_b, layer, *, seq_len, tt_pref=256):
    n, c = u.shape
    b = n // seq_len
    tt = _row_tile(seq_len, tt_pref)
    assert tt % CONV_HALO == 0
    nt = seq_len // tt
    hb = tt // CONV_HALO
    kern = functools.partial(_conv_kernel, tt=tt)
    vec = pl.BlockSpec((None, 1, c), lambda i, t: (layer, 0, 0))
    return pl.pallas_call(
        kern,
        grid=(b, nt),
        in_specs=[
            pl.BlockSpec((tt, c), lambda i, t: (i * nt + t, 0)),
            pl.BlockSpec((CONV_HALO, c), lambda i, t: (jnp.maximum((i * nt + t) * hb - 1, 0), 0)),
            pl.BlockSpec((None, CONV_HALO, c), lambda i, t: (i, 0, 0)),
            pl.BlockSpec((None, CONV_WIDTH, SUBLANES, c), lambda i, t: (layer, 0, 0, 0)),
            pl.BlockSpec((None, SUBLANES, c), lambda i, t: (layer, 0, 0)),
            vec, vec,
        ],
        out_specs=pl.BlockSpec((tt, c), lambda i, t: (i * nt + t, 0)),
        out_shape=jax.ShapeDtypeStruct((n, c), BF16),
        scratch_shapes=[pltpu.VMEM((c // LANES, tt + CONV_HALO, LANES), F32), pltpu.VMEM((tt, c), F32)],
        compiler_params=_cparams(2),
        name="conv_module",
    )(u, u, prefix, conv_w, conv_b, ln_g, ln_b)


def _pair_scores(q2, kb):
    lane = lax.broadcasted_iota(jnp.int32, (CHUNK, PAIR), 1)
    lo = lane < HEAD_DIM
    zero = jnp.zeros_like(q2)
    qs = jnp.concatenate([jnp.where(lo, q2, zero), jnp.where(lo, zero, q2)], axis=0)
    return lax.dot_general(qs, kb, (((1,), (1,)), ((), ())), preferred_element_type=F32)


def _pair_softmax_pv(s, bias, vb, n_invalid):
    s = s + bias
    if n_invalid is None:
        first = s[:, :LANES]
        col = lax.broadcasted_iota(jnp.int32, first.shape, 1)
        s = jnp.concatenate([jnp.where(col >= CHUNK, first, NEG), s[:, LANES:]], axis=1)
    else:
        col = lax.broadcasted_iota(jnp.int32, s.shape, 1)
        s = jnp.where(col >= n_invalid, s, NEG)
    m = jnp.max(s, axis=-1, keepdims=True)
    e = jnp.exp2(s - m)
    l = jnp.sum(e, axis=-1, keepdims=True)
    r = jnp.dot(e.astype(BF16), vb, preferred_element_type=F32)
    r = r * (1.0 / l)
    lane = lax.broadcasted_iota(jnp.int32, (CHUNK, PAIR), 1)
    return jnp.where(lane < HEAD_DIM, r[:CHUNK], r[CHUNK:])


def _attn_prompt_kernel(q_ref, k_ref, v_ref, bias_ref, o_ref, kp_ref, vp_ref, s_ref, *, seq_len, npairs):
    pad = BAND_PAD - CHUNK
    kp_ref[0:pad, :] = jnp.zeros((pad, kp_ref.shape[1]), BF16)
    vp_ref[0:pad, :] = jnp.zeros((pad, vp_ref.shape[1]), BF16)
    kp_ref[pad:, :] = k_ref[...]
    vp_ref[pad:, :] = v_ref[...]
    n_chunks = seq_len // CHUNK
    n_masked = min(PAST_CHUNKS, n_chunks)
    assert n_chunks % 2 == 0 and n_masked % 2 == 0

    def scores(c, slot):
        r0 = pl.multiple_of(c * CHUNK, CHUNK)
        for hp in range(npairs):
            cols = slice(hp * PAIR, (hp + 1) * PAIR)
            s_ref[slot, hp] = _pair_scores(q_ref[pl.ds(r0, CHUNK), cols], kp_ref[pl.ds(r0, BAND_PAD), cols])

    def finish(c, slot, masked):
        r0 = pl.multiple_of(c * CHUNK, CHUNK)
        n_invalid = (PAST_CHUNKS + 1 - c) * CHUNK if masked else None
        for hp in range(npairs):
            cols = slice(hp * PAIR, (hp + 1) * PAIR)
            o2 = _pair_softmax_pv(s_ref[slot, hp], bias_ref[hp], vp_ref[pl.ds(r0, BAND_PAD), cols], n_invalid)
            o_ref[pl.ds(r0, CHUNK), cols] = o2.astype(o_ref.dtype)

    def two_chunks(cc, masked):
        c = 2 * cc
        scores(c + 1, 1)
        finish(c, 0, masked)
        scores(jnp.minimum(c + 2, n_chunks - 1), 0)
        finish(c + 1, 1, masked)

    def masked_body(cc, carry):
        two_chunks(cc, True)
        return carry

    def plain_body(cc, carry):
        two_chunks(cc, False)
        return carry

    scores(0, 0)
    lax.fori_loop(0, n_masked // 2, masked_body, 0)
    lax.fori_loop(n_masked // 2, n_chunks // 2, plain_body, 0)


def _attn_prompt(qkv, bias, layer, *, seq_len, group_pairs=4):
    n, da = qkv.shape[0], qkv.shape[1] // 3
    b = n // seq_len
    gp = min(group_pairs, da // PAIR)
    gw = gp * PAIR
    assert da % gw == 0
    ng = da // gw
    kern = functools.partial(_attn_prompt_kernel, seq_len=seq_len, npairs=gp)
    part = lambda k: pl.BlockSpec((seq_len, gw), lambda i, g: (i, k * ng + g))
    return pl.pallas_call(
        kern,
        grid=(b, ng),
        in_specs=[part(0), part(1), part(2),
                  pl.BlockSpec((None, gp, PAIR, BAND_PAD), lambda i, g: (layer, g, 0, 0))],
        out_specs=part(0),
        out_shape=jax.ShapeDtypeStruct((n, da), BF16),
        scratch_shapes=[pltpu.VMEM((seq_len + BAND_PAD - CHUNK, gw), BF16),
                        pltpu.VMEM((seq_len + BAND_PAD - CHUNK, gw), BF16),
                        pltpu.VMEM((2, gp, PAIR, BAND_PAD), F32)],
        compiler_params=_cparams(2),
        name="attn_prompt",
    )(qkv, qkv, qkv, bias)


def _attn_sample_kernel(q_ref, k_ref, v_ref, kt_ref, vt_ref, ck_ref, cv_ref, bias_ref,
                        o_ref, nk_ref, nv_ref, kb_ref, vb_ref, *, npairs):
    keep = ck_ref.shape[0]
    da = ck_ref.shape[1]
    kb_ref[0:CHUNK, :] = jnp.zeros((CHUNK, da), BF16)
    vb_ref[0:CHUNK, :] = jnp.zeros((CHUNK, da), BF16)
    kb_ref[CHUNK:CHUNK + keep, :] = ck_ref[...].astype(BF16)
    vb_ref[CHUNK:CHUNK + keep, :] = cv_ref[...].astype(BF16)
    kb_ref[CHUNK + keep:, :] = k_ref[...]
    vb_ref[CHUNK + keep:, :] = v_ref[...]
    nk_ref[0:keep - CHUNK, :] = ck_ref[CHUNK:, :]
    nv_ref[0:keep - CHUNK, :] = cv_ref[CHUNK:, :]
    nk_ref[keep - CHUNK:, :] = kt_ref[...]
    nv_ref[keep - CHUNK:, :] = vt_ref[...]
    cols = [slice(hp * PAIR, (hp + 1) * PAIR) for hp in range(npairs)]
    ss = [_pair_scores(q_ref[:, c], kb_ref[:, c]) for c in cols]
    for hp, c in enumerate(cols):
        o_ref[:, c] = _pair_softmax_pv(ss[hp], bias_ref[hp], vb_ref[:, c], None).astype(o_ref.dtype)


def _attn_sample(qkv, qkv_f32, cache_k, cache_v, bias, layer):
    n, da = qkv.shape[0], qkv.shape[1] // 3
    b = n // CHUNK
    keep = cache_k.shape[2]
    assert keep == BAND_PAST
    npairs = da // PAIR
    kern = functools.partial(_attn_sample_kernel, npairs=npairs)
    part = lambda k: pl.BlockSpec((CHUNK, da), lambda i: (i, k))
    cache = pl.BlockSpec((None, None, keep, da), lambda i: (layer, i, 0, 0))
    newc = pl.BlockSpec((None, keep, da), lambda i: (i, 0, 0))
    return pl.pallas_call(
        kern,
        grid=(b,),
        in_specs=[part(0), part(1), part(2), part(1), part(2), cache, cache,
                  pl.BlockSpec((None, npairs, PAIR, BAND_PAD), lambda i: (layer, 0, 0, 0))],
        out_specs=[part(0), newc, newc],
        out_shape=[jax.ShapeDtypeStruct((n, da), BF16),
                   jax.ShapeDtypeStruct((b, keep, da), F32),
                   jax.ShapeDtypeStruct((b, keep, da), F32)],
        scratch_shapes=[pltpu.VMEM((BAND_PAD, da), BF16), pltpu.VMEM((BAND_PAD, da), BF16)],
        compiler_params=_cparams(1),
        name="attn_sample",
    )(qkv, qkv, qkv, qkv_f32, qkv_f32, cache_k, cache_v, bias)


def _band_bias(rel_bias):
    depth, h, _ = rel_bias.shape
    n_far = BAND + CHUNK - 1 - REL_CLIP + 1
    n_near = BAND_PAD + CHUNK - 1 - n_far
    far = jnp.broadcast_to(rel_bias[:, :, 2 * REL_CLIP:], (depth, h, n_far))
    near = lax.rev(rel_bias[:, :, 2 * REL_CLIP - n_near:2 * REL_CLIP], (2,))
    g = jnp.concatenate([far, near], axis=2)
    g = g.astype(F32) * LOG2E
    rows = [g[:, :, CHUNK - 1 - i:CHUNK - 1 - i + BAND_PAD] for i in range(CHUNK)]
    return jnp.stack(rows, axis=2).reshape(depth, h // 2, PAIR, BAND_PAD)


def _gatemix_kernel(h_ref, c_ref, a_ref, wco_ref, wo_ref, wg1_ref, wg2_ref, b1_ref, b2_ref, m_ref):
    h = h_ref[...]
    gc = _sigmoid(jnp.dot(h, wg1_ref[...], preferred_element_type=F32) + b1_ref[...])
    yc = jnp.dot(c_ref[...], wco_ref[...], preferred_element_type=F32)
    mc = gc * yc
    ga = _sigmoid(jnp.dot(h, wg2_ref[...], preferred_element_type=F32) + b2_ref[...])
    ya = jnp.dot(a_ref[...], wo_ref[...], preferred_element_type=F32)
    m_ref[...] = (mc + ga * ya).astype(BF16)


def _gatemix(h, cb, ab, wco, wo, wg, bg, layer, *, tm_pref=1024, tn=512):
    n, d = h.shape
    dc, da = cb.shape[1], ab.shape[1]
    tm = _row_tile(n, tm_pref)
    assert d % tn == 0
    nc = d // tn
    return pl.pallas_call(
        _gatemix_kernel,
        grid=(n // tm, nc),
        in_specs=[
            pl.BlockSpec((tm, d), lambda i, p: (i, 0)),
            pl.BlockSpec((tm, dc), lambda i, p: (i, 0)),
            pl.BlockSpec((tm, da), lambda i, p: (i, 0)),
            pl.BlockSpec((None, dc, tn), lambda i, p: (layer, 0, p)),
            pl.BlockSpec((None, da, tn), lambda i, p: (layer, 0, p)),
            pl.BlockSpec((None, d, tn), lambda i, p: (layer, 0, p)),
            pl.BlockSpec((None, d, tn), lambda i, p: (layer, 0, nc + p)),
            pl.BlockSpec((None, 1, tn), lambda i, p: (layer, 0, p)),
            pl.BlockSpec((None, 1, tn), lambda i, p: (layer, 0, nc + p)),
        ],
        out_specs=pl.BlockSpec((tm, tn), lambda i, p: (i, p)),
        out_shape=jax.ShapeDtypeStruct((n, d), BF16),
        compiler_params=_cparams(2),
        name="gatemix",
    )(h, cb, ab, wco, wo, wg, wg, bg, bg)


def _ffn_kernel(x_ref, m_ref, wout_ref, g_ref, wg_ref, wu_ref, wdn_ref, gf_ref, o_ref, h_ref, *, nf, final_norm):
    j = pl.program_id(1)

    @pl.when(j == 0)
    def _():
        x1 = x_ref[...] + jnp.dot(m_ref[...], wout_ref[...], preferred_element_type=F32)
        h_ref[...] = _rms_rows(x1, g_ref[...]).astype(BF16)
        o_ref[...] = x1

    h = h_ref[...]
    gate = jnp.dot(h, wg_ref[...], preferred_element_type=F32)
    up = jnp.dot(h, wu_ref[...], preferred_element_type=F32)
    act = ((gate * _sigmoid(gate)) * up).astype(BF16)
    o_ref[...] += jnp.dot(act, wdn_ref[...], preferred_element_type=F32)

    if final_norm:
        @pl.when(j == nf - 1)
        def _():
            o_ref[...] = _rms_rows(o_ref[...], gf_ref[...])


def _ffn(x, m, wout, g, wup, wdn, gf, layer, *, final_norm, tm_pref=512, tf=512):
    n, d = x.shape
    dff = wdn.shape[1]
    tm = _row_tile(n, tm_pref)
    assert dff % tf == 0
    nf = dff // tf
    kern = functools.partial(_ffn_kernel, nf=nf, final_norm=final_norm)
    return pl.pallas_call(
        kern,
        grid=(n // tm, nf),
        in_specs=[
            pl.BlockSpec((tm, d), lambda i, j: (i, 0)),
            pl.BlockSpec((tm, d), lambda i, j: (i, 0)),
            pl.BlockSpec((None, d, d), lambda i, j: (layer, 0, 0), pipeline_mode=pl.Buffered(1)),
            pl.BlockSpec((None, 1, d), lambda i, j: (layer, 0, 0)),
            pl.BlockSpec((None, d, tf), lambda i, j: (layer, 0, j)),
            pl.BlockSpec((None, d, tf), lambda i, j: (layer, 0, nf + j)),
            pl.BlockSpec((None, tf, d), lambda i, j: (layer, j, 0)),
            pl.BlockSpec((1, d), lambda i, j: (0, 0)),
        ],
        out_specs=pl.BlockSpec((tm, d), lambda i, j: (i, 0)),
        out_shape=jax.ShapeDtypeStruct((n, d), F32),
        scratch_shapes=[pltpu.VMEM((tm, d), BF16)],
        compiler_params=_cparams(2),
        name="ffn",
    )(x, m, wout, g, wup, wup, wdn, gf)


def kernel(x_prompt, x_sample, cache_k, cache_v, state_conv, norm_mix, w_in, conv_w, conv_b, ln_g, ln_b,
           w_conv_out, rel_bias, w_o, w_gate, b_gate, w_out, norm_ffn, w_up, w_down, norm_final):
    depth = w_in.shape[0]
    bp, tp, d = x_prompt.shape
    bs, ts, _ = x_sample.shape
    dc = conv_w.shape[2]
    da = w_o.shape[1]
    dff = w_down.shape[1]
    keep_s = cache_k.shape[2]
    nh = da // HEAD_DIM
    n_state = CONV_WIDTH - 1
    assert ts == CHUNK and tp % CHUNK == 0
    tf = 512 if dff % 512 == 0 else 256

    w_ab_b, w_qkv_b = w_in[:, :, :2 * dc].astype(BF16), w_in[:, :, 2 * dc:].astype(BF16)
    w_up_b, w_down_b, w_co_b, w_o_b, w_gate_b, w_out_b = (
        w.astype(BF16) for w in (w_up, w_down, w_conv_out, w_o, w_gate, w_out))
    row3 = lambda a: a.reshape(depth, 1, a.shape[-1])
    g_mix, g_ffn, cb3, lg3, lb3, bg3 = map(row3, (norm_mix, norm_ffn, conv_b, ln_g, ln_b, b_gate))
    g_fin = norm_final.reshape(1, d)
    bias = _band_bias(rel_bias)
    ck = cache_k.reshape(depth, bs, keep_s, da)
    cv = cache_v.reshape(depth, bs, keep_s, da)
    conv_p = (jnp.broadcast_to(conv_w[:, :, None, :], (depth, CONV_WIDTH, SUBLANES, dc)),
              jnp.broadcast_to(cb3, (depth, SUBLANES, dc)), lg3, lb3)

    xp = x_prompt.reshape(bp * tp, d)
    xs = x_sample.reshape(bs * ts, d)
    keep_p = min(BAND_PAST, tp)
    heads = lambda a, b, keep: a.reshape(b, keep, nh, HEAD_DIM)
    kp, vp, cp, ksl, vsl, csl = [], [], [], [], [], []
    qkv_scale = jnp.where(jnp.arange(3 * da) < da, QK_SCALE, 1.0).astype(F32).reshape(1, 3 * da)
    for l in range(depth):
        mix_w = (w_co_b, w_o_b, w_gate_b, bg3, l)
        ffn_w = (w_out_b, g_ffn, w_up_b, w_down_b, g_fin, l)
        last = l == depth - 1
        h, cact, qkv, tail, ust = _inproj(xp, g_mix, w_ab_b, w_qkv_b, qkv_scale, l, seq_len=tp, conv=conv_p)
        att = _attn_prompt(qkv, bias, l, seq_len=tp)
        xp = _ffn(xp, _gatemix(h, cact, att, *mix_w), *ffn_w, final_norm=last, tf=tf)
        kp.append(heads(tail[:, da:2 * da], bp, keep_p))
        vp.append(heads(tail[:, 2 * da:], bp, keep_p))
        cp.append(ust[:, CONV_HALO - n_state:])
        h, u, qkv, tail = _inproj(xs, g_mix, w_ab_b, w_qkv_b, qkv_scale, l, seq_len=ts)
        att, nk, nv = _attn_sample(qkv, tail, ck, cv, bias, l)
        pre_s = jnp.pad(state_conv[l], ((0, 0), (CONV_HALO - n_state, 0), (0, 0)))
        cact = _conv_module(u, pre_s, *conv_p, l, seq_len=ts)
        xs = _ffn(xs, _gatemix(h, cact, att, *mix_w), *ffn_w, final_norm=last, tf=tf)
        ksl.append(heads(nk, bs, keep_s))
        vsl.append(heads(nv, bs, keep_s))
        csl.append(u.reshape(bs, ts, dc)[:, ts - n_state:])
    return (xp.reshape(bp, tp, d), xs.reshape(bs, ts, d), jnp.stack(kp), jnp.stack(vp), jnp.stack(cp),
            jnp.stack(ksl), jnp.stack(vsl), jnp.stack(csl))
```

```python
import functools

import jax
import jax.numpy as jnp
from jax import lax
from jax.experimental import pallas as pl
from jax.experimental.pallas import tpu as pltpu

F32 = jnp.float32
BF16 = jnp.bfloat16

CHUNK = 64
PAST_CHUNKS = 8
BAND_PAST = PAST_CHUNKS * CHUNK
BAND = BAND_PAST + CHUNK
BAND_PAD = BAND + CHUNK
HEAD_DIM = 64
CONV_WIDTH = 31
CONV_HALO = 32
CONV_ROWS = 128
REL_CLIP = 128
EPS = 1e-6
NEG = -1e30
LOG2E = 1.4426950408889634
QK_SCALE = HEAD_DIM ** -0.5 * LOG2E

LANES = 128
SUBLANES = 8
PAIR = 2 * HEAD_DIM
VMEM_LIMIT_BYTES = 56 * 1024 * 1024
N_QKV_STEPS = 4


def _cparams(ndims):
    return pltpu.CompilerParams(dimension_semantics=("arbitrary",) * ndims,
                                vmem_limit_bytes=VMEM_LIMIT_BYTES)


def _sigmoid(x):
    return 1.0 / (1.0 + jnp.exp(-x))


def _rms_rows(x, g):
    ms = jnp.mean(x * x, axis=-1, keepdims=True)
    return (x * lax.rsqrt(ms + EPS)) * g


def _row_tile(n, pref):
    t = min(n, pref)
    assert n % t == 0, (n, t)
    return t


def _strips(c):
    assert c % LANES == 0
    return [slice(s * LANES, (s + 1) * LANES) for s in range(c // LANES)]


def _conv_taps(uext_ref, base, cw_ref, cb_ref, y_ref, rows):
    first = CONV_HALO - (CONV_WIDTH - 1)
    group = min(rows, CONV_ROWS)
    assert rows % group == 0 and group % SUBLANES == 0
    for s, lanes in enumerate(_strips(y_ref.shape[1])):
        for r in range(0, rows, group):
            accs = [cb_ref[:, lanes]] * (group // SUBLANES)
            for w in range(CONV_WIDTH):
                wt = cw_ref[w, :, lanes]
                accs = [a + uext_ref[s, pl.ds(base + r + t * SUBLANES + first + w, SUBLANES), :] * wt
                        for t, a in enumerate(accs)]
            for t, a in enumerate(accs):
                y_ref[r + t * SUBLANES:r + (t + 1) * SUBLANES, lanes] = a


def _ln_swish(y, lg, lb):
    mu = jnp.mean(y, axis=-1, keepdims=True)
    d = y - mu
    var = jnp.mean(d * d, axis=-1, keepdims=True)
    z = (d * lax.rsqrt(var + EPS)) * lg + lb
    return z * _sigmoid(z)


def _kv_pieces(t, tq, da):
    lo, hi = t * tq, (t + 1) * tq
    out = []
    for part in (0, 1):
        plo, phi = (1 + part) * da, (2 + part) * da
        a, b = max(lo, plo), min(hi, phi)
        if a < b:
            out.append((part, a - plo, b - plo, a - lo, b - lo))
    return out


def _inproj_kernel(*refs, nu, nq, tm, tu, tq, da, tail_rows, tail_pred, bps, fuse_conv, from_x):
    refs = list(refs)
    take = lambda k: [refs.pop(0) for _ in range(k)]
    if from_x:
        x_ref, g_ref = take(2)
    else:
        (h_ref,) = take(1)
    wa_ref, wb_ref, wq_ref, sc_ref = take(4)
    if fuse_conv:
        cw_ref, cb_ref, lg_ref, lb_ref = take(4)
    if from_x:
        (h_ref,) = take(1)
    cu_ref, qkv_ref, kt_ref, vt_ref = take(4)
    if fuse_conv:
        st_ref, uext_ref, y_ref = take(3)
    else:
        (uext_ref,) = take(1)
    i = pl.program_id(0)
    j = pl.program_id(1)
    ib = i % bps

    @pl.when(j == 0)
    def _():
        if from_x:
            h_ref[...] = _rms_rows(x_ref[...], g_ref[...]).astype(BF16)
        if fuse_conv:
            @pl.when(ib == 0)
            def _():
                uext_ref[:, 0:CONV_HALO, :] = jnp.zeros((uext_ref.shape[0], CONV_HALO, LANES), F32)

            @pl.when(ib != 0)
            def _():
                uext_ref[:, 0:CONV_HALO, :] = uext_ref[:, tm:tm + CONV_HALO, :]

    for jj in range(nu):
        @pl.when(j == jj)
        def _():
            h = h_ref[...]
            a = jnp.dot(h, wa_ref[...], preferred_element_type=F32)
            b = jnp.dot(h, wb_ref[...], preferred_element_type=F32)
            u = a * _sigmoid(b)
            for s, lanes in enumerate(_strips(tu)):
                uext_ref[jj * (tu // LANES) + s, CONV_HALO:, :] = u[:, lanes]

    is_tail = tail_pred(i)

    @pl.when(j >= nu)
    def _():
        p = jnp.dot(h_ref[...], wq_ref[...], preferred_element_type=F32)
        qkv_ref[...] = (p * sc_ref[...]).astype(BF16)
        if fuse_conv:
            rows = tm // nq
            r0 = pl.multiple_of((j - nu) * rows, rows)
            _conv_taps(uext_ref, r0, cw_ref, cb_ref, y_ref, rows)
            cu_ref[pl.ds(r0, rows), :] = _ln_swish(y_ref[...], lg_ref[...], lb_ref[...]).astype(BF16)

        for t in range(nq):
            pieces = _kv_pieces(t, tq, da)
            if pieces:
                @pl.when(is_tail & (j == nu + t))
                def _(pieces=pieces):
                    for part, dlo, dhi, slo, shi in pieces:
                        (vt_ref if part else kt_ref)[:, dlo:dhi] = p[tm - tail_rows:, slo:shi]

        if fuse_conv:
            @pl.when((j == nu) & (ib == bps - 1))
            def _():
                for s, lanes in enumerate(_strips(st_ref.shape[1])):
                    st_ref[:, lanes] = uext_ref[s, tm:tm + CONV_HALO, :]
        else:
            @pl.when(j == nu)
            def _():
                for s, lanes in enumerate(_strips(cu_ref.shape[1])):
                    cu_ref[:, lanes] = uext_ref[s, CONV_HALO:, :]


def _inproj(src, g, w_ab, w_qkv, qkv_scale, layer, *, seq_len, conv=None, from_x=True, tm_pref=512):
    n, d = src.shape
    dc, dqkv = w_ab.shape[2] // 2, w_qkv.shape[2]
    da = dqkv // 3
    tm = _row_tile(n, tm_pref)
    nu, nq = 2, N_QKV_STEPS
    tu, tq = dc // nu, dqkv // nq
    assert tu % LANES == 0 and tq % LANES == 0
    ni, nj = n // tm, nu + nq
    keep = min(BAND_PAST, seq_len)
    fuse_conv = conv is not None
    if seq_len % tm == 0:
        bps = seq_len // tm
        assert tm % keep == 0 or keep % tm == 0
        tail_rows = min(tm, keep)
        ntb = keep // tail_rows
        n_tail = (n // seq_len) * keep

        def tail_blk(i):
            return (i // bps) * ntb + jnp.clip(i % bps - (bps - ntb), 0, ntb - 1)

        def tail_pred(i):
            return (i % bps) >= (bps - ntb)
    else:
        assert tm % seq_len == 0 and keep == seq_len and not fuse_conv
        bps, tail_rows, n_tail = 1, tm, n

        def tail_blk(i):
            return i

        def tail_pred(i):
            return i >= 0

    kern = functools.partial(_inproj_kernel, nu=nu, nq=nq, tm=tm, tu=tu, tq=tq, da=da, tail_rows=tail_rows,
                             tail_pred=tail_pred, bps=bps, fuse_conv=fuse_conv, from_x=from_x)
    ucol = lambda j: jnp.minimum(j, nu - 1)
    qcol = lambda j: jnp.clip(j - nu, 0, nq - 1)
    vec = pl.BlockSpec((None, 1, dc), lambda i, j: (layer, 0, 0))
    rows = pl.BlockSpec((tm, d), lambda i, j: (i, 0))
    in_specs = [rows, pl.BlockSpec((None, 1, d), lambda i, j: (layer, 0, 0))] if from_x else [rows]
    args = [src, g] if from_x else [src]
    in_specs += [
        pl.BlockSpec((None, d, tu), lambda i, j: (layer, 0, ucol(j))),
        pl.BlockSpec((None, d, tu), lambda i, j: (layer, 0, nu + ucol(j))),
        pl.BlockSpec((None, d, tq), lambda i, j: (layer, 0, qcol(j))),
        pl.BlockSpec((1, tq), lambda i, j: (0, qcol(j))),
    ]
    args += [w_ab, w_ab, w_qkv, qkv_scale]
    tail = pl.BlockSpec((tail_rows, da), lambda i, j: (tail_blk(i), 0))
    out_specs = ([rows] if from_x else []) + [
        pl.BlockSpec((tm, dc), lambda i, j: (i, 0)),
        pl.BlockSpec((tm, tq), lambda i, j: (i, qcol(j))),
        tail, tail,
    ]
    out_shape = ([jax.ShapeDtypeStruct((n, d), BF16)] if from_x else []) + [
        jax.ShapeDtypeStruct((n, dc), BF16 if fuse_conv else F32),
        jax.ShapeDtypeStruct((n, dqkv), BF16),
        jax.ShapeDtypeStruct((n_tail, da), F32),
        jax.ShapeDtypeStruct((n_tail, da), F32),
    ]
    scratch = [pltpu.VMEM((dc // LANES, tm + CONV_HALO, LANES), F32)]
    if fuse_conv:
        in_specs += [pl.BlockSpec((None, CONV_WIDTH, SUBLANES, dc), lambda i, j: (layer, 0, 0, 0)),
                     pl.BlockSpec((None, SUBLANES, dc), lambda i, j: (layer, 0, 0)), vec, vec]
        args += list(conv)
        out_specs.append(pl.BlockSpec((None, CONV_HALO, dc), lambda i, j: (i // bps, 0, 0)))
        out_shape.append(jax.ShapeDtypeStruct((n // seq_len, CONV_HALO, dc), F32))
        scratch.append(pltpu.VMEM((tm // nq, dc), F32))
    return pl.pallas_call(
        kern,
        grid=(ni, nj),
        in_specs=in_specs,
        out_specs=out_specs,
        out_shape=out_shape,
        scratch_shapes=scratch,
        compiler_params=_cparams(2),
        name="inproj_conv" if fuse_conv else "inproj",
    )(*args)


def _conv_kernel(u_ref, halo_ref, pre_ref, cw_ref, cb_ref, lg_ref, lb_ref, o_ref, ext_ref, y_ref, *, tt):
    t = pl.program_id(1)

    strips = _strips(u_ref.shape[1])

    @pl.when(t == 0)
    def _():
        for s, lanes in enumerate(strips):
            ext_ref[s, 0:CONV_HALO, :] = pre_ref[:, lanes]

    @pl.when(t > 0)
    def _():
        for s, lanes in enumerate(strips):
            ext_ref[s, 0:CONV_HALO, :] = halo_ref[:, lanes]

    for s, lanes in enumerate(strips):
        ext_ref[s, CONV_HALO:, :] = u_ref[:, lanes]
    _conv_taps(ext_ref, 0, cw_ref, cb_ref, y_ref, tt)
    o_ref[...] = _ln_swish(y_ref[...], lg_ref[...], lb_ref[...]).astype(o_ref.dtype)


def _conv_module(u, prefix, conv_w, conv_b, ln_g, ln_b, layer, *, seq_len, tt_pref=256):
    n, c = u.shape
    b = n // seq_len
    tt = _row_tile(seq_len, tt_pref)
    assert tt % CONV_HALO == 0
    nt = seq_len // tt
    hb = tt // CONV_HALO
    kern = functools.partial(_conv_kernel, tt=tt)
    vec = pl.BlockSpec((None, 1, c), lambda i, t: (layer, 0, 0))
    return pl.pallas_call(
        kern,
        grid=(b, nt),
        in_specs=[
            pl.BlockSpec((tt, c), lambda i, t: (i * nt + t, 0)),
            pl.BlockSpec((CONV_HALO, c), lambda i, t: (jnp.maximum((i * nt + t) * hb - 1, 0), 0)),
            pl.BlockSpec((None, CONV_HALO, c), lambda i, t: (i, 0, 0)),
            pl.BlockSpec((None, CONV_WIDTH, SUBLANES, c), lambda i, t: (layer, 0, 0, 0)),
            pl.BlockSpec((None, SUBLANES, c), lambda i, t: (layer, 0, 0)),
            vec, vec,
        ],
        out_specs=pl.BlockSpec((tt, c), lambda i, t: (i * nt + t, 0)),
        out_shape=jax.ShapeDtypeStruct((n, c), BF16),
        scratch_shapes=[pltpu.VMEM((c // LANES, tt + CONV_HALO, LANES), F32), pltpu.VMEM((tt, c), F32)],
        compiler_params=_cparams(2),
        name="conv_module",
    )(u, u, prefix, conv_w, conv_b, ln_g, ln_b)


def _pair_scores(q2, kb):
    lane = lax.broadcasted_iota(jnp.int32, (CHUNK, PAIR), 1)
    lo = lane < HEAD_DIM
    zero = jnp.zeros_like(q2)
    qs = jnp.concatenate([jnp.where(lo, q2, zero), jnp.where(lo, zero, q2)], axis=0)
    return lax.dot_general(qs, kb, (((1,), (1,)), ((), ())), preferred_element_type=F32)


def _pair_softmax_pv(s, bias, vb, n_invalid):
    s = s + bias
    if n_invalid is None:
        first = s[:, :LANES]
        col = lax.broadcasted_iota(jnp.int32, first.shape, 1)
        s = jnp.concatenate([jnp.where(col >= CHUNK, first, NEG), s[:, LANES:]], axis=1)
    else:
        col = lax.broadcasted_iota(jnp.int32, s.shape, 1)
        s = jnp.where(col >= n_invalid, s, NEG)
    m = jnp.max(s, axis=-1, keepdims=True)
    e = jnp.exp2(s - m)
    l = jnp.sum(e, axis=-1, keepdims=True)
    r = jnp.dot(e.astype(BF16), vb, preferred_element_type=F32)
    r = r * (1.0 / l)
    lane = lax.broadcasted_iota(jnp.int32, (CHUNK, PAIR), 1)
    return jnp.where(lane < HEAD_DIM, r[:CHUNK], r[CHUNK:])


def _attn_prompt_kernel(q_ref, k_ref, v_ref, bias_ref, o_ref, kp_ref, vp_ref, s_ref, *, seq_len, npairs):
    pad = BAND_PAD - CHUNK
    kp_ref[0:pad, :] = jnp.zeros((pad, kp_ref.shape[1]), BF16)
    vp_ref[0:pad, :] = jnp.zeros((pad, vp_ref.shape[1]), BF16)
    kp_ref[pad:, :] = k_ref[...]
    vp_ref[pad:, :] = v_ref[...]
    n_chunks = seq_len // CHUNK
    n_masked = min(PAST_CHUNKS, n_chunks)
    assert n_chunks % 2 == 0 and n_masked % 2 == 0

    def scores(c, slot):
        r0 = pl.multiple_of(c * CHUNK, CHUNK)
        for hp in range(npairs):
            cols = slice(hp * PAIR, (hp + 1) * PAIR)
            s_ref[slot, hp] = _pair_scores(q_ref[pl.ds(r0, CHUNK), cols], kp_ref[pl.ds(r0, BAND_PAD), cols])

    def finish(c, slot, masked):
        r0 = pl.multiple_of(c * CHUNK, CHUNK)
        n_invalid = (PAST_CHUNKS + 1 - c) * CHUNK if masked else None
        for hp in range(npairs):
            cols = slice(hp * PAIR, (hp + 1) * PAIR)
            o2 = _pair_softmax_pv(s_ref[slot, hp], bias_ref[hp], vp_ref[pl.ds(r0, BAND_PAD), cols], n_invalid)
            o_ref[pl.ds(r0, CHUNK), cols] = o2.astype(o_ref.dtype)

    def two_chunks(cc, masked):
        c = 2 * cc
        scores(c + 1, 1)
        finish(c, 0, masked)
        scores(jnp.minimum(c + 2, n_chunks - 1), 0)
        finish(c + 1, 1, masked)

    def masked_body(cc, carry):
        two_chunks(cc, True)
        return carry

    def plain_body(cc, carry):
        two_chunks(cc, False)
        return carry

    scores(0, 0)
    lax.fori_loop(0, n_masked // 2, masked_body, 0)
    lax.fori_loop(n_masked // 2, n_chunks // 2, plain_body, 0)


def _attn_prompt(qkv, bias, layer, *, seq_len, group_pairs=4):
    n, da = qkv.shape[0], qkv.shape[1] // 3
    b = n // seq_len
    gp = min(group_pairs, da // PAIR)
    gw = gp * PAIR
    assert da % gw == 0
    ng = da // gw
    kern = functools.partial(_attn_prompt_kernel, seq_len=seq_len, npairs=gp)
    part = lambda k: pl.BlockSpec((seq_len, gw), lambda i, g: (i, k * ng + g))
    return pl.pallas_call(
        kern,
        grid=(b, ng),
        in_specs=[part(0), part(1), part(2),
                  pl.BlockSpec((None, gp, PAIR, BAND_PAD), lambda i, g: (layer, g, 0, 0))],
        out_specs=part(0),
        out_shape=jax.ShapeDtypeStruct((n, da), BF16),
        scratch_shapes=[pltpu.VMEM((seq_len + BAND_PAD - CHUNK, gw), BF16),
                        pltpu.VMEM((seq_len + BAND_PAD - CHUNK, gw), BF16),
                        pltpu.VMEM((2, gp, PAIR, BAND_PAD), F32)],
        compiler_params=_cparams(2),
        name="attn_prompt",
    )(qkv, qkv, qkv, bias)


def _attn_sample_kernel(q_ref, k_ref, v_ref, kt_ref, vt_ref, ck_ref, cv_ref, bias_ref,
                        o_ref, nk_ref, nv_ref, kb_ref, vb_ref, *, npairs):
    keep = ck_ref.shape[0]
    da = ck_ref.shape[1]
    kb_ref[0:CHUNK, :] = jnp.zeros((CHUNK, da), BF16)
    vb_ref[0:CHUNK, :] = jnp.zeros((CHUNK, da), BF16)
    kb_ref[CHUNK:CHUNK + keep, :] = ck_ref[...].astype(BF16)
    vb_ref[CHUNK:CHUNK + keep, :] = cv_ref[...].astype(BF16)
    kb_ref[CHUNK + keep:, :] = k_ref[...]
    vb_ref[CHUNK + keep:, :] = v_ref[...]
    nk_ref[0:keep - CHUNK, :] = ck_ref[CHUNK:, :]
    nv_ref[0:keep - CHUNK, :] = cv_ref[CHUNK:, :]
    nk_ref[keep - CHUNK:, :] = kt_ref[...]
    nv_ref[keep - CHUNK:, :] = vt_ref[...]
    cols = [slice(hp * PAIR, (hp + 1) * PAIR) for hp in range(npairs)]
    ss = [_pair_scores(q_ref[:, c], kb_ref[:, c]) for c in cols]
    for hp, c in enumerate(cols):
        o_ref[:, c] = _pair_softmax_pv(ss[hp], bias_ref[hp], vb_ref[:, c], None).astype(o_ref.dtype)


def _attn_sample(qkv, kt, vt, cache_k, cache_v, bias, layer):
    n, da = qkv.shape[0], qkv.shape[1] // 3
    b = n // CHUNK
    keep = cache_k.shape[2]
    assert keep == BAND_PAST
    npairs = da // PAIR
    kern = functools.partial(_attn_sample_kernel, npairs=npairs)
    part = lambda k: pl.BlockSpec((CHUNK, da), lambda i: (i, k))
    cache = pl.BlockSpec((None, None, keep, da), lambda i: (layer, i, 0, 0))
    newc = pl.BlockSpec((None, keep, da), lambda i: (i, 0, 0))
    return pl.pallas_call(
        kern,
        grid=(b,),
        in_specs=[part(0), part(1), part(2), part(0), part(0), cache, cache,
                  pl.BlockSpec((None, npairs, PAIR, BAND_PAD), lambda i: (layer, 0, 0, 0))],
        out_specs=[part(0), newc, newc],
        out_shape=[jax.ShapeDtypeStruct((n, da), BF16),
                   jax.ShapeDtypeStruct((b, keep, da), F32),
                   jax.ShapeDtypeStruct((b, keep, da), F32)],
        scratch_shapes=[pltpu.VMEM((BAND_PAD, da), BF16), pltpu.VMEM((BAND_PAD, da), BF16)],
        compiler_params=_cparams(1),
        name="attn_sample",
    )(qkv, qkv, qkv, kt, vt, cache_k, cache_v, bias)


def _band_bias(rel_bias):
    depth, h, _ = rel_bias.shape
    n_far = BAND + CHUNK - 1 - REL_CLIP + 1
    n_near = BAND_PAD + CHUNK - 1 - n_far
    far = jnp.broadcast_to(rel_bias[:, :, 2 * REL_CLIP:], (depth, h, n_far))
    near = lax.rev(rel_bias[:, :, 2 * REL_CLIP - n_near:2 * REL_CLIP], (2,))
    g = jnp.concatenate([far, near], axis=2)
    g = g.astype(F32) * LOG2E
    rows = [g[:, :, CHUNK - 1 - i:CHUNK - 1 - i + BAND_PAD] for i in range(CHUNK)]
    return jnp.stack(rows, axis=2).reshape(depth, h // 2, PAIR, BAND_PAD)


def _gatemix_kernel(h_ref, c_ref, a_ref, wco_ref, wo_ref, wg1_ref, wg2_ref, b1_ref, b2_ref, m_ref):
    h = h_ref[...]
    gc = _sigmoid(jnp.dot(h, wg1_ref[...], preferred_element_type=F32) + b1_ref[...])
    yc = jnp.dot(c_ref[...], wco_ref[...], preferred_element_type=F32)
    mc = gc * yc
    ga = _sigmoid(jnp.dot(h, wg2_ref[...], preferred_element_type=F32) + b2_ref[...])
    ya = jnp.dot(a_ref[...], wo_ref[...], preferred_element_type=F32)
    m_ref[...] = (mc + ga * ya).astype(BF16)


def _gatemix(h, cb, ab, wco, wo, wg, bg, layer, *, tm_pref=1024, tn=512):
    n, d = h.shape
    dc, da = cb.shape[1], ab.shape[1]
    tm = _row_tile(n, tm_pref)
    assert d % tn == 0
    nc = d // tn
    return pl.pallas_call(
        _gatemix_kernel,
        grid=(n // tm, nc),
        in_specs=[
            pl.BlockSpec((tm, d), lambda i, p: (i, 0)),
            pl.BlockSpec((tm, dc), lambda i, p: (i, 0)),
            pl.BlockSpec((tm, da), lambda i, p: (i, 0)),
            pl.BlockSpec((None, dc, tn), lambda i, p: (layer, 0, p)),
            pl.BlockSpec((None, da, tn), lambda i, p: (layer, 0, p)),
            pl.BlockSpec((None, d, tn), lambda i, p: (layer, 0, p)),
            pl.BlockSpec((None, d, tn), lambda i, p: (layer, 0, nc + p)),
            pl.BlockSpec((None, 1, tn), lambda i, p: (layer, 0, p)),
            pl.BlockSpec((None, 1, tn), lambda i, p: (layer, 0, nc + p)),
        ],
        out_specs=pl.BlockSpec((tm, tn), lambda i, p: (i, p)),
        out_shape=jax.ShapeDtypeStruct((n, d), BF16),
        compiler_params=_cparams(2),
        name="gatemix",
    )(h, cb, ab, wco, wo, wg, wg, bg, bg)


def _ffn_kernel(x_ref, m_ref, wout_ref, g_ref, wg_ref, wu_ref, wdn_ref, gp_ref, o_ref, *rest, nf, final_norm):
    hn_ref, h_ref = (None, rest[0]) if final_norm else rest
    j = pl.program_id(1)

    @pl.when(j == 0)
    def _():
        x1 = x_ref[...] + jnp.dot(m_ref[...], wout_ref[...], preferred_element_type=F32)
        h_ref[...] = _rms_rows(x1, g_ref[...]).astype(BF16)
        o_ref[...] = x1

    h = h_ref[...]
    gate = jnp.dot(h, wg_ref[...], preferred_element_type=F32)
    up = jnp.dot(h, wu_ref[...], preferred_element_type=F32)
    act = ((gate * _sigmoid(gate)) * up).astype(BF16)
    o_ref[...] += jnp.dot(act, wdn_ref[...], preferred_element_type=F32)

    @pl.when(j == nf - 1)
    def _():
        y = _rms_rows(o_ref[...], gp_ref[...])
        if final_norm:
            o_ref[...] = y
        else:
            hn_ref[...] = y.astype(BF16)


def _ffn(x, m, wout, g, wup, wdn, g_post, post_idx, layer, *, final_norm, tm_pref=512, tf=512):
    n, d = x.shape
    dff = wdn.shape[1]
    tm = _row_tile(n, tm_pref)
    assert dff % tf == 0
    nf = dff // tf
    kern = functools.partial(_ffn_kernel, nf=nf, final_norm=final_norm)
    rows = pl.BlockSpec((tm, d), lambda i, j: (i, 0))
    out_specs, out_shape = [rows], [jax.ShapeDtypeStruct((n, d), F32)]
    if not final_norm:
        out_specs.append(rows)
        out_shape.append(jax.ShapeDtypeStruct((n, d), BF16))
    return pl.pallas_call(
        kern,
        grid=(n // tm, nf),
        in_specs=[
            rows,
            rows,
            pl.BlockSpec((None, d, d), lambda i, j: (layer, 0, 0), pipeline_mode=pl.Buffered(1)),
            pl.BlockSpec((None, 1, d), lambda i, j: (layer, 0, 0)),
            pl.BlockSpec((None, d, tf), lambda i, j: (layer, 0, j)),
            pl.BlockSpec((None, d, tf), lambda i, j: (layer, 0, nf + j)),
            pl.BlockSpec((None, tf, d), lambda i, j: (layer, j, 0)),
            pl.BlockSpec((None, 1, d), lambda i, j: (post_idx, 0, 0)),
        ],
        out_specs=out_specs,
        out_shape=out_shape,
        scratch_shapes=[pltpu.VMEM((tm, d), BF16)],
        compiler_params=_cparams(2),
        name="ffn",
    )(x, m, wout, g, wup, wup, wdn, g_post)


def kernel(x_prompt, x_sample, cache_k, cache_v, state_conv, norm_mix, w_in, conv_w, conv_b, ln_g, ln_b,
           w_conv_out, rel_bias, w_o, w_gate, b_gate, w_out, norm_ffn, w_up, w_down, norm_final):
    depth = w_in.shape[0]
    bp, tp, d = x_prompt.shape
    bs, ts, _ = x_sample.shape
    dc = conv_w.shape[2]
    da = w_o.shape[1]
    dff = w_down.shape[1]
    keep_s = cache_k.shape[2]
    nh = da // HEAD_DIM
    n_state = CONV_WIDTH - 1
    assert ts == CHUNK and tp % CHUNK == 0
    tf = 512 if dff % 512 == 0 else 256

    w_ab_b, w_qkv_b = w_in[:, :, :2 * dc].astype(BF16), w_in[:, :, 2 * dc:].astype(BF16)
    w_up_b, w_down_b, w_co_b, w_o_b, w_gate_b, w_out_b = (
        w.astype(BF16) for w in (w_up, w_down, w_conv_out, w_o, w_gate, w_out))
    row3 = lambda a: a.reshape(depth, 1, a.shape[-1])
    g_mix, g_ffn, cb3, lg3, lb3, bg3 = map(row3, (norm_mix, norm_ffn, conv_b, ln_g, ln_b, b_gate))
    g_fin = norm_final.reshape(1, 1, d)
    bias = _band_bias(rel_bias)
    ck = cache_k.reshape(depth, bs, keep_s, da)
    cv = cache_v.reshape(depth, bs, keep_s, da)
    conv_p = (jnp.broadcast_to(conv_w[:, :, None, :], (depth, CONV_WIDTH, SUBLANES, dc)),
              jnp.broadcast_to(cb3, (depth, SUBLANES, dc)), lg3, lb3)

    xp = x_prompt.reshape(bp * tp, d)
    xs = x_sample.reshape(bs * ts, d)
    keep_p = min(BAND_PAST, tp)
    heads = lambda a, b, keep: a.reshape(b, keep, nh, HEAD_DIM)
    kp, vp, cp, ksl, vsl, csl = [], [], [], [], [], []
    qkv_scale = jnp.where(jnp.arange(3 * da) < da, QK_SCALE, 1.0).astype(F32).reshape(1, 3 * da)
    hp = hs = None
    for l in range(depth):
        first, last = l == 0, l == depth - 1
        in_w = (w_ab_b, w_qkv_b, qkv_scale, l)
        mix_w = (w_co_b, w_o_b, w_gate_b, bg3, l)
        ffn_w = (w_out_b, g_ffn, w_up_b, w_down_b) + ((g_fin, 0, l) if last else (g_mix, l + 1, l))
        if first:
            hp, cact, qkv, kt, vt, ust = _inproj(xp, g_mix, *in_w, seq_len=tp, conv=conv_p)
        else:
            cact, qkv, kt, vt, ust = _inproj(hp, None, *in_w, seq_len=tp, conv=conv_p, from_x=False, tm_pref=1024)
        att = _attn_prompt(qkv, bias, l, seq_len=tp)
        out = _ffn(xp, _gatemix(hp, cact, att, *mix_w), *ffn_w, final_norm=last, tf=tf)
        xp, hp = (out[0], None) if last else out
        kp.append(heads(kt, bp, keep_p))
        vp.append(heads(vt, bp, keep_p))
        cp.append(ust[:, CONV_HALO - n_state:])
        if first:
            hs, u, qkv, kt, vt = _inproj(xs, g_mix, *in_w, seq_len=ts)
        else:
            u, qkv, kt, vt = _inproj(hs, None, *in_w, seq_len=ts, from_x=False)
        att, nk, nv = _attn_sample(qkv, kt, vt, ck, cv, bias, l)
        pre_s = jnp.pad(state_conv[l], ((0, 0), (CONV_HALO - n_state, 0), (0, 0)))
        cact = _conv_module(u, pre_s, *conv_p, l, seq_len=ts)
        out = _ffn(xs, _gatemix(hs, cact, att, *mix_w), *ffn_w, final_norm=last, tf=tf)
        xs, hs = (out[0], None) if last else out
        ksl.append(heads(nk, bs, keep_s))
        vsl.append(heads(nv, bs, keep_s))
        csl.append(u.reshape(bs, ts, dc)[:, ts - n_state:])
    return (xp.reshape(bp, tp, d), xs.reshape(bs, ts, d), jnp.stack(kp), jnp.stack(vp), jnp.stack(cp),
            jnp.stack(ksl), jnp.stack(vsl), jnp.stack(csl))
```

```python
import functools

import jax
import jax.numpy as jnp
from jax import lax
from jax.experimental import pallas as pl
from jax.experimental.pallas import tpu as pltpu

F32 = jnp.float32
BF16 = jnp.bfloat16

CHUNK = 64
PAST_CHUNKS = 8
BAND_PAST = PAST_CHUNKS * CHUNK
BAND = BAND_PAST + CHUNK
BAND_PAD = BAND + CHUNK
HEAD_DIM = 64
CONV_WIDTH = 31
CONV_HALO = 32
CONV_ROWS = 128
REL_CLIP = 128
EPS = 1e-6
NEG = -1e30
LOG2E = 1.4426950408889634
QK_SCALE = HEAD_DIM ** -0.5 * LOG2E

LANES = 128
SUBLANES = 8
PAIR = 2 * HEAD_DIM
VMEM_LIMIT_BYTES = 56 * 1024 * 1024
N_QKV_STEPS = 4


def _cparams(ndims):
    return pltpu.CompilerParams(dimension_semantics=("arbitrary",) * ndims,
                                vmem_limit_bytes=VMEM_LIMIT_BYTES)


def _sigmoid(x):
    return 1.0 / (1.0 + jnp.exp(-x))


def _rms_rows(x, g):
    ms = jnp.mean(x * x, axis=-1, keepdims=True)
    return (x * lax.rsqrt(ms + EPS)) * g


def _row_tile(n, pref):
    t = min(n, pref)
    assert n % t == 0, (n, t)
    return t


def _strips(c):
    assert c % LANES == 0
    return [slice(s * LANES, (s + 1) * LANES) for s in range(c // LANES)]


def _conv_taps(uext_ref, base, cw_ref, cb_ref, y_ref, rows):
    first = CONV_HALO - (CONV_WIDTH - 1)
    group = min(rows, CONV_ROWS)
    assert rows % group == 0 and group % SUBLANES == 0
    for s, lanes in enumerate(_strips(y_ref.shape[1])):
        for r in range(0, rows, group):
            accs = [cb_ref[:, lanes]] * (group // SUBLANES)
            for w in range(CONV_WIDTH):
                wt = cw_ref[w, :, lanes]
                accs = [a + uext_ref[s, pl.ds(base + r + t * SUBLANES + first + w, SUBLANES), :] * wt
                        for t, a in enumerate(accs)]
            for t, a in enumerate(accs):
                y_ref[r + t * SUBLANES:r + (t + 1) * SUBLANES, lanes] = a


def _ln_swish(y, lg, lb):
    mu = jnp.mean(y, axis=-1, keepdims=True)
    d = y - mu
    var = jnp.mean(d * d, axis=-1, keepdims=True)
    z = (d * lax.rsqrt(var + EPS)) * lg + lb
    return z * _sigmoid(z)


def _kv_pieces(t, tq, da):
    lo, hi = t * tq, (t + 1) * tq
    out = []
    for part in (0, 1):
        plo, phi = (1 + part) * da, (2 + part) * da
        a, b = max(lo, plo), min(hi, phi)
        if a < b:
            out.append((part, a - plo, b - plo, a - lo, b - lo))
    return out


def _inproj_kernel(*refs, nu, nq, tm, tu, tq, da, tail_rows, tail_pred, bps, fuse_conv, from_x):
    refs = list(refs)
    take = lambda k: [refs.pop(0) for _ in range(k)]
    if from_x:
        x_ref, g_ref = take(2)
    else:
        (h_ref,) = take(1)
    wa_ref, wb_ref, wq_ref, sc_ref = take(4)
    if fuse_conv:
        cw_ref, cb_ref, lg_ref, lb_ref = take(4)
    if from_x:
        (h_ref,) = take(1)
    cu_ref, qkv_ref, kt_ref, vt_ref = take(4)
    if fuse_conv:
        st_ref, uext_ref, y_ref = take(3)
    else:
        (uext_ref,) = take(1)
    i = pl.program_id(0)
    j = pl.program_id(1)
    ib = i % bps

    @pl.when(j == 0)
    def _():
        if from_x:
            h_ref[...] = _rms_rows(x_ref[...], g_ref[...]).astype(BF16)
        if fuse_conv:
            @pl.when(ib == 0)
            def _():
                uext_ref[:, 0:CONV_HALO, :] = jnp.zeros((uext_ref.shape[0], CONV_HALO, LANES), F32)

            @pl.when(ib != 0)
            def _():
                uext_ref[:, 0:CONV_HALO, :] = uext_ref[:, tm:tm + CONV_HALO, :]

    for jj in range(nu):
        @pl.when(j == jj)
        def _():
            h = h_ref[...]
            a = jnp.dot(h, wa_ref[...], preferred_element_type=F32)
            b = jnp.dot(h, wb_ref[...], preferred_element_type=F32)
            u = a * _sigmoid(b)
            for s, lanes in enumerate(_strips(tu)):
                uext_ref[jj * (tu // LANES) + s, CONV_HALO:, :] = u[:, lanes]

    is_tail = tail_pred(i)

    @pl.when(j >= nu)
    def _():
        p = jnp.dot(h_ref[...], wq_ref[...], preferred_element_type=F32)
        qkv_ref[...] = (p * sc_ref[...]).astype(BF16)
        if fuse_conv:
            rows = tm // nq
            r0 = pl.multiple_of((j - nu) * rows, rows)
            _conv_taps(uext_ref, r0, cw_ref, cb_ref, y_ref, rows)
            cu_ref[pl.ds(r0, rows), :] = _ln_swish(y_ref[...], lg_ref[...], lb_ref[...]).astype(BF16)

        for t in range(nq):
            pieces = _kv_pieces(t, tq, da)
            if pieces:
                @pl.when(is_tail & (j == nu + t))
                def _(pieces=pieces):
                    for part, dlo, dhi, slo, shi in pieces:
                        (vt_ref if part else kt_ref)[:, dlo:dhi] = p[tm - tail_rows:, slo:shi]

        if fuse_conv:
            @pl.when((j == nu) & (ib == bps - 1))
            def _():
                for s, lanes in enumerate(_strips(st_ref.shape[1])):
                    st_ref[:, lanes] = uext_ref[s, tm:tm + CONV_HALO, :]
        else:
            @pl.when(j == nu)
            def _():
                for s, lanes in enumerate(_strips(cu_ref.shape[1])):
                    cu_ref[:, lanes] = uext_ref[s, CONV_HALO:, :]


def _inproj(src, g, w_ab, w_qkv, qkv_scale, layer, *, seq_len, conv=None, from_x=True, tm_pref=512):
    n, d = src.shape
    dc, dqkv = w_ab.shape[2] // 2, w_qkv.shape[2]
    da = dqkv // 3
    tm = _row_tile(n, tm_pref)
    nu, nq = 2, N_QKV_STEPS
    tu, tq = dc // nu, dqkv // nq
    assert tu % LANES == 0 and tq % LANES == 0
    ni, nj = n // tm, nu + nq
    keep = min(BAND_PAST, seq_len)
    fuse_conv = conv is not None
    if seq_len % tm == 0:
        bps = seq_len // tm
        assert tm % keep == 0 or keep % tm == 0
        tail_rows = min(tm, keep)
        ntb = keep // tail_rows
        n_tail = (n // seq_len) * keep

        def tail_blk(i):
            return (i // bps) * ntb + jnp.clip(i % bps - (bps - ntb), 0, ntb - 1)

        def tail_pred(i):
            return (i % bps) >= (bps - ntb)
    else:
        assert tm % seq_len == 0 and keep == seq_len and not fuse_conv
        bps, tail_rows, n_tail = 1, tm, n

        def tail_blk(i):
            return i

        def tail_pred(i):
            return i >= 0

    kern = functools.partial(_inproj_kernel, nu=nu, nq=nq, tm=tm, tu=tu, tq=tq, da=da, tail_rows=tail_rows,
                             tail_pred=tail_pred, bps=bps, fuse_conv=fuse_conv, from_x=from_x)
    ucol = lambda j: jnp.minimum(j, nu - 1)
    qcol = lambda j: jnp.clip(j - nu, 0, nq - 1)
    vec = pl.BlockSpec((None, 1, dc), lambda i, j: (layer, 0, 0))
    rows = pl.BlockSpec((tm, d), lambda i, j: (i, 0))
    in_specs = [rows, pl.BlockSpec((None, 1, d), lambda i, j: (layer, 0, 0))] if from_x else [rows]
    args = [src, g] if from_x else [src]
    in_specs += [
        pl.BlockSpec((None, d, tu), lambda i, j: (layer, 0, ucol(j))),
        pl.BlockSpec((None, d, tu), lambda i, j: (layer, 0, nu + ucol(j))),
        pl.BlockSpec((None, d, tq), lambda i, j: (layer, 0, qcol(j))),
        pl.BlockSpec((1, tq), lambda i, j: (0, qcol(j))),
    ]
    args += [w_ab, w_ab, w_qkv, qkv_scale]
    tail = pl.BlockSpec((tail_rows, da), lambda i, j: (tail_blk(i), 0))
    out_specs = ([rows] if from_x else []) + [
        pl.BlockSpec((tm, dc), lambda i, j: (i, 0)),
        pl.BlockSpec((tm, tq), lambda i, j: (i, qcol(j))),
        tail, tail,
    ]
    out_shape = ([jax.ShapeDtypeStruct((n, d), BF16)] if from_x else []) + [
        jax.ShapeDtypeStruct((n, dc), BF16 if fuse_conv else F32),
        jax.ShapeDtypeStruct((n, dqkv), BF16),
        jax.ShapeDtypeStruct((n_tail, da), F32),
        jax.ShapeDtypeStruct((n_tail, da), F32),
    ]
    scratch = [pltpu.VMEM((dc // LANES, tm + CONV_HALO, LANES), F32)]
    if fuse_conv:
        in_specs += [pl.BlockSpec((None, CONV_WIDTH, SUBLANES, dc), lambda i, j: (layer, 0, 0, 0)),
                     pl.BlockSpec((None, SUBLANES, dc), lambda i, j: (layer, 0, 0)), vec, vec]
        args += list(conv)
        out_specs.append(pl.BlockSpec((None, CONV_HALO, dc), lambda i, j: (i // bps, 0, 0)))
        out_shape.append(jax.ShapeDtypeStruct((n // seq_len, CONV_HALO, dc), F32))
        scratch.append(pltpu.VMEM((tm // nq, dc), F32))
    return pl.pallas_call(
        kern,
        grid=(ni, nj),
        in_specs=in_specs,
        out_specs=out_specs,
        out_shape=out_shape,
        scratch_shapes=scratch,
        compiler_params=_cparams(2),
        name="inproj_conv" if fuse_conv else "inproj",
    )(*args)


def _conv_kernel(u_ref, halo_ref, pre_ref, cw_ref, cb_ref, lg_ref, lb_ref, o_ref, ext_ref, y_ref, *, tt):
    t = pl.program_id(1)

    strips = _strips(u_ref.shape[1])

    @pl.when(t == 0)
    def _():
        for s, lanes in enumerate(strips):
            ext_ref[s, 0:CONV_HALO, :] = pre_ref[:, lanes]

    @pl.when(t > 0)
    def _():
        for s, lanes in enumerate(strips):
            ext_ref[s, 0:CONV_HALO, :] = halo_ref[:, lanes]

    for s, lanes in enumerate(strips):
        ext_ref[s, CONV_HALO:, :] = u_ref[:, lanes]
    _conv_taps(ext_ref, 0, cw_ref, cb_ref, y_ref, tt)
    o_ref[...] = _ln_swish(y_ref[...], lg_ref[...], lb_ref[...]).astype(o_ref.dtype)


def _conv_module(u, prefix, conv_w, conv_b, ln_g, ln_b, layer, *, seq_len, tt_pref=256):
    n, c = u.shape
    b = n // seq_len
    tt = _row_tile(seq_len, tt_pref)
    assert tt % CONV_HALO == 0
    nt = seq_len // tt
    hb = tt // CONV_HALO
    kern = functools.partial(_conv_kernel, tt=tt)
    vec = pl.BlockSpec((None, 1, c), lambda i, t: (layer, 0, 0))
    return pl.pallas_call(
        kern,
        grid=(b, nt),
        in_specs=[
            pl.BlockSpec((tt, c), lambda i, t: (i * nt + t, 0)),
            pl.BlockSpec((CONV_HALO, c), lambda i, t: (jnp.maximum((i * nt + t) * hb - 1, 0), 0)),
            pl.BlockSpec((None, CONV_HALO, c), lambda i, t: (i, 0, 0)),
            pl.BlockSpec((None, CONV_WIDTH, SUBLANES, c), lambda i, t: (layer, 0, 0, 0)),
            pl.BlockSpec((None, SUBLANES, c), lambda i, t: (layer, 0, 0)),
            vec, vec,
        ],
        out_specs=pl.BlockSpec((tt, c), lambda i, t: (i * nt + t, 0)),
        out_shape=jax.ShapeDtypeStruct((n, c), BF16),
        scratch_shapes=[pltpu.VMEM((c // LANES, tt + CONV_HALO, LANES), F32), pltpu.VMEM((tt, c), F32)],
        compiler_params=_cparams(2),
        name="conv_module",
    )(u, u, prefix, conv_w, conv_b, ln_g, ln_b)


def _pair_scores(q2, kb):
    lane = lax.broadcasted_iota(jnp.int32, (CHUNK, PAIR), 1)
    lo = lane < HEAD_DIM
    zero = jnp.zeros_like(q2)
    qs = jnp.concatenate([jnp.where(lo, q2, zero), jnp.where(lo, zero, q2)], axis=0)
    return lax.dot_general(qs, kb, (((1,), (1,)), ((), ())), preferred_element_type=F32)


def _pair_softmax_pv(s, bias, vb, n_invalid):
    s = s + bias
    if n_invalid is None:
        first = s[:, :LANES]
        col = lax.broadcasted_iota(jnp.int32, first.shape, 1)
        s = jnp.concatenate([jnp.where(col >= CHUNK, first, NEG), s[:, LANES:]], axis=1)
    else:
        col = lax.broadcasted_iota(jnp.int32, s.shape, 1)
        s = jnp.where(col >= n_invalid, s, NEG)
    m = jnp.max(s, axis=-1, keepdims=True)
    e = jnp.exp2(s - m)
    l = jnp.sum(e, axis=-1, keepdims=True)
    r = jnp.dot(e.astype(BF16), vb, preferred_element_type=F32)
    r = r * (1.0 / l)
    lane = lax.broadcasted_iota(jnp.int32, (CHUNK, PAIR), 1)
    return jnp.where(lane < HEAD_DIM, r[:CHUNK], r[CHUNK:])


def _attn_prompt_kernel(q_ref, k_ref, v_ref, bias_ref, o_ref, kp_ref, vp_ref, s_ref, *, seq_len, npairs):
    pad = BAND_PAD - CHUNK
    kp_ref[0:pad, :] = jnp.zeros((pad, kp_ref.shape[1]), BF16)
    vp_ref[0:pad, :] = jnp.zeros((pad, vp_ref.shape[1]), BF16)
    kp_ref[pad:, :] = k_ref[...]
    vp_ref[pad:, :] = v_ref[...]
    n_chunks = seq_len // CHUNK
    n_masked = min(PAST_CHUNKS, n_chunks)
    assert n_chunks % 2 == 0 and n_masked % 2 == 0

    def scores(c, slot):
        r0 = pl.multiple_of(c * CHUNK, CHUNK)
        for hp in range(npairs):
            cols = slice(hp * PAIR, (hp + 1) * PAIR)
            s_ref[slot, hp] = _pair_scores(q_ref[pl.ds(r0, CHUNK), cols], kp_ref[pl.ds(r0, BAND_PAD), cols])

    def finish(c, slot, masked):
        r0 = pl.multiple_of(c * CHUNK, CHUNK)
        n_invalid = (PAST_CHUNKS + 1 - c) * CHUNK if masked else None
        for hp in range(npairs):
            cols = slice(hp * PAIR, (hp + 1) * PAIR)
            o2 = _pair_softmax_pv(s_ref[slot, hp], bias_ref[hp], vp_ref[pl.ds(r0, BAND_PAD), cols], n_invalid)
            o_ref[pl.ds(r0, CHUNK), cols] = o2.astype(o_ref.dtype)

    def two_chunks(cc, masked):
        c = 2 * cc
        scores(c + 1, 1)
        finish(c, 0, masked)
        scores(jnp.minimum(c + 2, n_chunks - 1), 0)
        finish(c + 1, 1, masked)

    def masked_body(cc, carry):
        two_chunks(cc, True)
        return carry

    def plain_body(cc, carry):
        two_chunks(cc, False)
        return carry

    scores(0, 0)
    lax.fori_loop(0, n_masked // 2, masked_body, 0)
    lax.fori_loop(n_masked // 2, n_chunks // 2, plain_body, 0)


def _attn_prompt(qkv, bias, layer, *, seq_len, group_pairs=2):
    n, da = qkv.shape[0], qkv.shape[1] // 3
    b = n // seq_len
    gp = min(group_pairs, da // PAIR)
    gw = gp * PAIR
    assert da % gw == 0
    ng = da // gw
    kern = functools.partial(_attn_prompt_kernel, seq_len=seq_len, npairs=gp)
    part = lambda k: pl.BlockSpec((seq_len, gw), lambda i, g: (i, k * ng + g))
    return pl.pallas_call(
        kern,
        grid=(b, ng),
        in_specs=[part(0), part(1), part(2),
                  pl.BlockSpec((None, gp, PAIR, BAND_PAD), lambda i, g: (layer, g, 0, 0))],
        out_specs=part(0),
        out_shape=jax.ShapeDtypeStruct((n, da), BF16),
        scratch_shapes=[pltpu.VMEM((seq_len + BAND_PAD - CHUNK, gw), BF16),
                        pltpu.VMEM((seq_len + BAND_PAD - CHUNK, gw), BF16),
                        pltpu.VMEM((2, gp, PAIR, BAND_PAD), F32)],
        compiler_params=_cparams(2),
        name="attn_prompt",
    )(qkv, qkv, qkv, bias)


def _attn_sample_kernel(q_ref, k_ref, v_ref, kt_ref, vt_ref, ck_ref, cv_ref, bias_ref,
                        o_ref, nk_ref, nv_ref, kb_ref, vb_ref, *, npairs):
    keep = ck_ref.shape[0]
    da = ck_ref.shape[1]
    kb_ref[0:CHUNK, :] = jnp.zeros((CHUNK, da), BF16)
    vb_ref[0:CHUNK, :] = jnp.zeros((CHUNK, da), BF16)
    kb_ref[CHUNK:CHUNK + keep, :] = ck_ref[...].astype(BF16)
    vb_ref[CHUNK:CHUNK + keep, :] = cv_ref[...].astype(BF16)
    kb_ref[CHUNK + keep:, :] = k_ref[...]
    vb_ref[CHUNK + keep:, :] = v_ref[...]
    nk_ref[0:keep - CHUNK, :] = ck_ref[CHUNK:, :]
    nv_ref[0:keep - CHUNK, :] = cv_ref[CHUNK:, :]
    nk_ref[keep - CHUNK:, :] = kt_ref[...]
    nv_ref[keep - CHUNK:, :] = vt_ref[...]
    cols = [slice(hp * PAIR, (hp + 1) * PAIR) for hp in range(npairs)]
    ss = [_pair_scores(q_ref[:, c], kb_ref[:, c]) for c in cols]
    for hp, c in enumerate(cols):
        o_ref[:, c] = _pair_softmax_pv(ss[hp], bias_ref[hp], vb_ref[:, c], None).astype(o_ref.dtype)


def _attn_sample(qkv, kt, vt, cache_k, cache_v, bias, layer):
    n, da = qkv.shape[0], qkv.shape[1] // 3
    b = n // CHUNK
    keep = cache_k.shape[2]
    assert keep == BAND_PAST
    npairs = da // PAIR
    kern = functools.partial(_attn_sample_kernel, npairs=npairs)
    part = lambda k: pl.BlockSpec((CHUNK, da), lambda i: (i, k))
    cache = pl.BlockSpec((None, None, keep, da), lambda i: (layer, i, 0, 0))
    newc = pl.BlockSpec((None, keep, da), lambda i: (i, 0, 0))
    return pl.pallas_call(
        kern,
        grid=(b,),
        in_specs=[part(0), part(1), part(2), part(0), part(0), cache, cache,
                  pl.BlockSpec((None, npairs, PAIR, BAND_PAD), lambda i: (layer, 0, 0, 0))],
        out_specs=[part(0), newc, newc],
        out_shape=[jax.ShapeDtypeStruct((n, da), BF16),
                   jax.ShapeDtypeStruct((b, keep, da), F32),
                   jax.ShapeDtypeStruct((b, keep, da), F32)],
        scratch_shapes=[pltpu.VMEM((BAND_PAD, da), BF16), pltpu.VMEM((BAND_PAD, da), BF16)],
        compiler_params=_cparams(1),
        name="attn_sample",
    )(qkv, qkv, qkv, kt, vt, cache_k, cache_v, bias)


def _band_bias(rel_bias):
    depth, h, _ = rel_bias.shape
    n_far = BAND + CHUNK - 1 - REL_CLIP + 1
    n_near = BAND_PAD + CHUNK - 1 - n_far
    far = jnp.broadcast_to(rel_bias[:, :, 2 * REL_CLIP:], (depth, h, n_far))
    near = lax.rev(rel_bias[:, :, 2 * REL_CLIP - n_near:2 * REL_CLIP], (2,))
    g = jnp.concatenate([far, near], axis=2)
    g = g.astype(F32) * LOG2E
    rows = [g[:, :, CHUNK - 1 - i:CHUNK - 1 - i + BAND_PAD] for i in range(CHUNK)]
    return jnp.stack(rows, axis=2).reshape(depth, h // 2, PAIR, BAND_PAD)


def _gatemix_kernel(h_ref, c_ref, a_ref, wco_ref, wo_ref, wg1_ref, wg2_ref, b1_ref, b2_ref, m_ref):
    h = h_ref[...]
    gc = _sigmoid(jnp.dot(h, wg1_ref[...], preferred_element_type=F32) + b1_ref[...])
    yc = jnp.dot(c_ref[...], wco_ref[...], preferred_element_type=F32)
    mc = gc * yc
    ga = _sigmoid(jnp.dot(h, wg2_ref[...], preferred_element_type=F32) + b2_ref[...])
    ya = jnp.dot(a_ref[...], wo_ref[...], preferred_element_type=F32)
    m_ref[...] = (mc + ga * ya).astype(BF16)


def _gatemix(h, cb, ab, wco, wo, wg, bg, layer, *, tm_pref=1024, tn=512):
    n, d = h.shape
    dc, da = cb.shape[1], ab.shape[1]
    tm = _row_tile(n, tm_pref)
    assert d % tn == 0
    nc = d // tn
    return pl.pallas_call(
        _gatemix_kernel,
        grid=(n // tm, nc),
        in_specs=[
            pl.BlockSpec((tm, d), lambda i, p: (i, 0)),
            pl.BlockSpec((tm, dc), lambda i, p: (i, 0)),
            pl.BlockSpec((tm, da), lambda i, p: (i, 0)),
            pl.BlockSpec((None, dc, tn), lambda i, p: (layer, 0, p)),
            pl.BlockSpec((None, da, tn), lambda i, p: (layer, 0, p)),
            pl.BlockSpec((None, d, tn), lambda i, p: (layer, 0, p)),
            pl.BlockSpec((None, d, tn), lambda i, p: (layer, 0, nc + p)),
            pl.BlockSpec((None, 1, tn), lambda i, p: (layer, 0, p)),
            pl.BlockSpec((None, 1, tn), lambda i, p: (layer, 0, nc + p)),
        ],
        out_specs=pl.BlockSpec((tm, tn), lambda i, p: (i, p)),
        out_shape=jax.ShapeDtypeStruct((n, d), BF16),
        compiler_params=_cparams(2),
        name="gatemix",
    )(h, cb, ab, wco, wo, wg, wg, bg, bg)


def _ffn_kernel(x_ref, m_ref, wout_ref, g_ref, wg_ref, wu_ref, wdn_ref, gp_ref, o_ref, *rest, nf, final_norm):
    hn_ref, h_ref = (None, rest[0]) if final_norm else rest
    j = pl.program_id(1)

    @pl.when(j == 0)
    def _():
        x1 = x_ref[...] + jnp.dot(m_ref[...], wout_ref[...], preferred_element_type=F32)
        h_ref[...] = _rms_rows(x1, g_ref[...]).astype(BF16)
        o_ref[...] = x1

    h = h_ref[...]
    gate = jnp.dot(h, wg_ref[...], preferred_element_type=F32)
    up = jnp.dot(h, wu_ref[...], preferred_element_type=F32)
    act = ((gate * _sigmoid(gate)) * up).astype(BF16)
    o_ref[...] += jnp.dot(act, wdn_ref[...], preferred_element_type=F32)

    @pl.when(j == nf - 1)
    def _():
        y = _rms_rows(o_ref[...], gp_ref[...])
        if final_norm:
            o_ref[...] = y
        else:
            hn_ref[...] = y.astype(BF16)


def _ffn(x, m, wout, g, wup, wdn, g_post, post_idx, layer, *, final_norm, tm_pref=512, tf=512):
    n, d = x.shape
    dff = wdn.shape[1]
    tm = _row_tile(n, tm_pref)
    assert dff % tf == 0
    nf = dff // tf
    kern = functools.partial(_ffn_kernel, nf=nf, final_norm=final_norm)
    rows = pl.BlockSpec((tm, d), lambda i, j: (i, 0))
    out_specs, out_shape = [rows], [jax.ShapeDtypeStruct((n, d), F32)]
    if not final_norm:
        out_specs.append(rows)
        out_shape.append(jax.ShapeDtypeStruct((n, d), BF16))
    return pl.pallas_call(
        kern,
        grid=(n // tm, nf),
        in_specs=[
            rows,
            rows,
            pl.BlockSpec((None, d, d), lambda i, j: (layer, 0, 0), pipeline_mode=pl.Buffered(1)),
            pl.BlockSpec((None, 1, d), lambda i, j: (layer, 0, 0)),
            pl.BlockSpec((None, d, tf), lambda i, j: (layer, 0, j)),
            pl.BlockSpec((None, d, tf), lambda i, j: (layer, 0, nf + j)),
            pl.BlockSpec((None, tf, d), lambda i, j: (layer, j, 0)),
            pl.BlockSpec((None, 1, d), lambda i, j: (post_idx, 0, 0)),
        ],
        out_specs=out_specs,
        out_shape=out_shape,
        scratch_shapes=[pltpu.VMEM((tm, d), BF16)],
        compiler_params=_cparams(2),
        name="ffn",
    )(x, m, wout, g, wup, wup, wdn, g_post)


def kernel(x_prompt, x_sample, cache_k, cache_v, state_conv, norm_mix, w_in, conv_w, conv_b, ln_g, ln_b,
           w_conv_out, rel_bias, w_o, w_gate, b_gate, w_out, norm_ffn, w_up, w_down, norm_final):
    depth = w_in.shape[0]
    bp, tp, d = x_prompt.shape
    bs, ts, _ = x_sample.shape
    dc = conv_w.shape[2]
    da = w_o.shape[1]
    dff = w_down.shape[1]
    keep_s = cache_k.shape[2]
    nh = da // HEAD_DIM
    n_state = CONV_WIDTH - 1
    assert ts == CHUNK and tp % CHUNK == 0
    tf = 512 if dff % 512 == 0 else 256

    w_ab_b, w_qkv_b = w_in[:, :, :2 * dc].astype(BF16), w_in[:, :, 2 * dc:].astype(BF16)
    w_up_b, w_down_b, w_co_b, w_o_b, w_gate_b, w_out_b = (
        w.astype(BF16) for w in (w_up, w_down, w_conv_out, w_o, w_gate, w_out))
    row3 = lambda a: a.reshape(depth, 1, a.shape[-1])
    g_mix, g_ffn, cb3, lg3, lb3, bg3 = map(row3, (norm_mix, norm_ffn, conv_b, ln_g, ln_b, b_gate))
    g_fin = norm_final.reshape(1, 1, d)
    bias = _band_bias(rel_bias)
    ck = cache_k.reshape(depth, bs, keep_s, da)
    cv = cache_v.reshape(depth, bs, keep_s, da)
    conv_p = (jnp.broadcast_to(conv_w[:, :, None, :], (depth, CONV_WIDTH, SUBLANES, dc)),
              jnp.broadcast_to(cb3, (depth, SUBLANES, dc)), lg3, lb3)

    xp = x_prompt.reshape(bp * tp, d)
    xs = x_sample.reshape(bs * ts, d)
    keep_p = min(BAND_PAST, tp)
    heads = lambda a, b, keep: a.reshape(b, keep, nh, HEAD_DIM)
    kp, vp, cp, ksl, vsl, csl = [], [], [], [], [], []
    qkv_scale = jnp.where(jnp.arange(3 * da) < da, QK_SCALE, 1.0).astype(F32).reshape(1, 3 * da)
    hp = hs = None
    for l in range(depth):
        first, last = l == 0, l == depth - 1
        in_w = (w_ab_b, w_qkv_b, qkv_scale, l)
        mix_w = (w_co_b, w_o_b, w_gate_b, bg3, l)
        ffn_w = (w_out_b, g_ffn, w_up_b, w_down_b) + ((g_fin, 0, l) if last else (g_mix, l + 1, l))
        if first:
            hp, cact, qkv, kt, vt, ust = _inproj(xp, g_mix, *in_w, seq_len=tp, conv=conv_p)
        else:
            cact, qkv, kt, vt, ust = _inproj(hp, None, *in_w, seq_len=tp, conv=conv_p, from_x=False, tm_pref=1024)
        att = _attn_prompt(qkv, bias, l, seq_len=tp)
        out = _ffn(xp, _gatemix(hp, cact, att, *mix_w), *ffn_w, final_norm=last, tf=tf)
        xp, hp = (out[0], None) if last else out
        kp.append(heads(kt, bp, keep_p))
        vp.append(heads(vt, bp, keep_p))
        cp.append(ust[:, CONV_HALO - n_state:])
        if first:
            hs, u, qkv, kt, vt = _inproj(xs, g_mix, *in_w, seq_len=ts)
        else:
            u, qkv, kt, vt = _inproj(hs, None, *in_w, seq_len=ts, from_x=False)
        att, nk, nv = _attn_sample(qkv, kt, vt, ck, cv, bias, l)
        pre_s = jnp.pad(state_conv[l], ((0, 0), (CONV_HALO - n_state, 0), (0, 0)))
        cact = _conv_module(u, pre_s, *conv_p, l, seq_len=ts)
        out = _ffn(xs, _gatemix(hs, cact, att, *mix_w), *ffn_w, final_norm=last, tf=tf)
        xs, hs = (out[0], None) if last else out
        ksl.append(heads(nk, bs, keep_s))
        vsl.append(heads(nv, bs, keep_s))
        csl.append(u.reshape(bs, ts, dc)[:, ts - n_state:])
    return (xp.reshape(bp, tp, d), xs.reshape(bs, ts, d), jnp.stack(kp), jnp.stack(vp), jnp.stack(cp),
            jnp.stack(ksl), jnp.stack(vsl), jnp.stack(csl))
```

```python
import functools

import jax
import jax.numpy as jnp
from jax import lax
from jax.experimental import pallas as pl
from jax.experimental.pallas import tpu as pltpu

F32 = jnp.float32
BF16 = jnp.bfloat16

CHUNK = 64
PAST_CHUNKS = 8
BAND_PAST = PAST_CHUNKS * CHUNK
BAND = BAND_PAST + CHUNK
BAND_PAD = BAND + CHUNK
HEAD_DIM = 64
CONV_WIDTH = 31
CONV_HALO = 32
CONV_ROWS = 32
REL_CLIP = 128
EPS = 1e-6
NEG = -1e30
LOG2E = 1.4426950408889634
QK_SCALE = HEAD_DIM ** -0.5 * LOG2E

LANES = 128
SUBLANES = 8
PAIR = 2 * HEAD_DIM
VMEM_LIMIT_BYTES = 56 * 1024 * 1024
N_QKV_STEPS = 4


def _cparams(ndims):
    return pltpu.CompilerParams(dimension_semantics=("arbitrary",) * ndims,
                                vmem_limit_bytes=VMEM_LIMIT_BYTES)


def _sigmoid(x):
    return 1.0 / (1.0 + jnp.exp(-x))


def _rms_rows(x, g):
    ms = jnp.mean(x * x, axis=-1, keepdims=True)
    return (x * lax.rsqrt(ms + EPS)) * g


def _row_tile(n, pref):
    t = min(n, pref)
    assert n % t == 0, (n, t)
    return t


def _strips(c):
    assert c % LANES == 0
    return [slice(s * LANES, (s + 1) * LANES) for s in range(c // LANES)]


def _conv_taps(uext_ref, base, cw_ref, cb_ref, y_ref, rows):
    first = CONV_HALO - (CONV_WIDTH - 1)
    group = min(rows, CONV_ROWS)
    assert rows % group == 0 and group % SUBLANES == 0
    prev = None
    for s, lanes in enumerate(_strips(y_ref.shape[1])):
        for r in range(0, rows, group):
            init = cb_ref[:, lanes]
            if prev is not None:
                zero = lax.shift_right_logical(lax.shift_right_logical(
                    lax.bitcast_convert_type(prev, jnp.uint32), jnp.uint32(16)), jnp.uint32(16))
                init = lax.bitcast_convert_type(lax.bitcast_convert_type(init, jnp.uint32) | zero, F32)
            accs = [init] * (group // SUBLANES)
            for w in range(CONV_WIDTH):
                wt = cw_ref[w, :, lanes]
                accs = [a + uext_ref[s, pl.ds(base + r + t * SUBLANES + first + w, SUBLANES), :] * wt
                        for t, a in enumerate(accs)]
            for t, a in enumerate(accs):
                y_ref[r + t * SUBLANES:r + (t + 1) * SUBLANES, lanes] = a
            prev = accs[0]


def _ln_swish(y, lg, lb):
    mu = jnp.mean(y, axis=-1, keepdims=True)
    d = y - mu
    var = jnp.mean(d * d, axis=-1, keepdims=True)
    z = (d * lax.rsqrt(var + EPS)) * lg + lb
    return z * _sigmoid(z)


def _kv_pieces(t, tq, da):
    lo, hi = t * tq, (t + 1) * tq
    out = []
    for part in (0, 1):
        plo, phi = (1 + part) * da, (2 + part) * da
        a, b = max(lo, plo), min(hi, phi)
        if a < b:
            out.append((part, a - plo, b - plo, a - lo, b - lo))
    return out


def _inproj_kernel(*refs, nu, nq, tm, tu, tq, da, tail_rows, tail_pred, bps, fuse_conv, from_x):
    refs = list(refs)
    take = lambda k: [refs.pop(0) for _ in range(k)]
    if from_x:
        x_ref, g_ref = take(2)
    else:
        (h_ref,) = take(1)
    wa_ref, wb_ref, wq_ref, sc_ref = take(4)
    if fuse_conv:
        cw_ref, cb_ref, lg_ref, lb_ref = take(4)
    if from_x:
        (h_ref,) = take(1)
    cu_ref, qkv_ref, kt_ref, vt_ref = take(4)
    if fuse_conv:
        st_ref, uext_ref, y_ref = take(3)
    else:
        (uext_ref,) = take(1)
    i = pl.program_id(0)
    j = pl.program_id(1)
    ib = i % bps

    @pl.when(j == 0)
    def _():
        if from_x:
            h_ref[...] = _rms_rows(x_ref[...], g_ref[...]).astype(BF16)
        if fuse_conv:
            @pl.when(ib == 0)
            def _():
                uext_ref[:, 0:CONV_HALO, :] = jnp.zeros((uext_ref.shape[0], CONV_HALO, LANES), F32)

            @pl.when(ib != 0)
            def _():
                uext_ref[:, 0:CONV_HALO, :] = uext_ref[:, tm:tm + CONV_HALO, :]

    for jj in range(nu):
        @pl.when(j == jj)
        def _():
            h = h_ref[...]
            a = jnp.dot(h, wa_ref[...], preferred_element_type=F32)
            b = jnp.dot(h, wb_ref[...], preferred_element_type=F32)
            u = a * _sigmoid(b)
            for s, lanes in enumerate(_strips(tu)):
                uext_ref[jj * (tu // LANES) + s, CONV_HALO:, :] = u[:, lanes]

    is_tail = tail_pred(i)

    @pl.when(j >= nu)
    def _():
        p = jnp.dot(h_ref[...], wq_ref[...], preferred_element_type=F32)
        qkv_ref[...] = (p * sc_ref[...]).astype(BF16)
        if fuse_conv:
            rows = tm // nq
            r0 = pl.multiple_of((j - nu) * rows, rows)
            _conv_taps(uext_ref, r0, cw_ref, cb_ref, y_ref, rows)
            cu_ref[pl.ds(r0, rows), :] = _ln_swish(y_ref[...], lg_ref[...], lb_ref[...]).astype(BF16)

        for t in range(nq):
            pieces = _kv_pieces(t, tq, da)
            if pieces:
                @pl.when(is_tail & (j == nu + t))
                def _(pieces=pieces):
                    for part, dlo, dhi, slo, shi in pieces:
                        (vt_ref if part else kt_ref)[:, dlo:dhi] = p[tm - tail_rows:, slo:shi]

        if fuse_conv:
            @pl.when((j == nu) & (ib == bps - 1))
            def _():
                for s, lanes in enumerate(_strips(st_ref.shape[1])):
                    st_ref[:, lanes] = uext_ref[s, tm:tm + CONV_HALO, :]
        else:
            @pl.when(j == nu)
            def _():
                for s, lanes in enumerate(_strips(cu_ref.shape[1])):
                    cu_ref[:, lanes] = uext_ref[s, CONV_HALO:, :]


def _inproj(src, g, w_ab, w_qkv, qkv_scale, layer, *, seq_len, conv=None, from_x=True, tm_pref=512):
    n, d = src.shape
    dc, dqkv = w_ab.shape[2] // 2, w_qkv.shape[2]
    da = dqkv // 3
    tm = _row_tile(n, tm_pref)
    nu, nq = 2, N_QKV_STEPS
    tu, tq = dc // nu, dqkv // nq
    assert tu % LANES == 0 and tq % LANES == 0
    ni, nj = n // tm, nu + nq
    keep = min(BAND_PAST, seq_len)
    fuse_conv = conv is not None
    if seq_len % tm == 0:
        bps = seq_len // tm
        assert tm % keep == 0 or keep % tm == 0
        tail_rows = min(tm, keep)
        ntb = keep // tail_rows
        n_tail = (n // seq_len) * keep

        def tail_blk(i):
            return (i // bps) * ntb + jnp.clip(i % bps - (bps - ntb), 0, ntb - 1)

        def tail_pred(i):
            return (i % bps) >= (bps - ntb)
    else:
        assert tm % seq_len == 0 and keep == seq_len and not fuse_conv
        bps, tail_rows, n_tail = 1, tm, n

        def tail_blk(i):
            return i

        def tail_pred(i):
            return i >= 0

    kern = functools.partial(_inproj_kernel, nu=nu, nq=nq, tm=tm, tu=tu, tq=tq, da=da, tail_rows=tail_rows,
                             tail_pred=tail_pred, bps=bps, fuse_conv=fuse_conv, from_x=from_x)
    ucol = lambda j: jnp.minimum(j, nu - 1)
    qcol = lambda j: jnp.clip(j - nu, 0, nq - 1)
    vec = pl.BlockSpec((None, 1, dc), lambda i, j: (layer, 0, 0))
    rows = pl.BlockSpec((tm, d), lambda i, j: (i, 0))
    in_specs = [rows, pl.BlockSpec((None, 1, d), lambda i, j: (layer, 0, 0))] if from_x else [rows]
    args = [src, g] if from_x else [src]
    in_specs += [
        pl.BlockSpec((None, d, tu), lambda i, j: (layer, 0, ucol(j))),
        pl.BlockSpec((None, d, tu), lambda i, j: (layer, 0, nu + ucol(j))),
        pl.BlockSpec((None, d, tq), lambda i, j: (layer, 0, qcol(j))),
        pl.BlockSpec((1, tq), lambda i, j: (0, qcol(j))),
    ]
    args += [w_ab, w_ab, w_qkv, qkv_scale]
    tail = pl.BlockSpec((tail_rows, da), lambda i, j: (tail_blk(i), 0))
    out_specs = ([rows] if from_x else []) + [
        pl.BlockSpec((tm, dc), lambda i, j: (i, 0)),
        pl.BlockSpec((tm, tq), lambda i, j: (i, qcol(j))),
        tail, tail,
    ]
    out_shape = ([jax.ShapeDtypeStruct((n, d), BF16)] if from_x else []) + [
        jax.ShapeDtypeStruct((n, dc), BF16 if fuse_conv else F32),
        jax.ShapeDtypeStruct((n, dqkv), BF16),
        jax.ShapeDtypeStruct((n_tail, da), F32),
        jax.ShapeDtypeStruct((n_tail, da), F32),
    ]
    scratch = [pltpu.VMEM((dc // LANES, tm + CONV_HALO, LANES), F32)]
    if fuse_conv:
        in_specs += [pl.BlockSpec((None, CONV_WIDTH, SUBLANES, dc), lambda i, j: (layer, 0, 0, 0)),
                     pl.BlockSpec((None, SUBLANES, dc), lambda i, j: (layer, 0, 0)), vec, vec]
        args += list(conv)
        out_specs.append(pl.BlockSpec((None, CONV_HALO, dc), lambda i, j: (i // bps, 0, 0)))
        out_shape.append(jax.ShapeDtypeStruct((n // seq_len, CONV_HALO, dc), F32))
        scratch.append(pltpu.VMEM((tm // nq, dc), F32))
    return pl.pallas_call(
        kern,
        grid=(ni, nj),
        in_specs=in_specs,
        out_specs=out_specs,
        out_shape=out_shape,
        scratch_shapes=scratch,
        compiler_params=_cparams(2),
        name="inproj_conv" if fuse_conv else "inproj",
    )(*args)


def _conv_kernel(u_ref, halo_ref, pre_ref, cw_ref, cb_ref, lg_ref, lb_ref, o_ref, ext_ref, y_ref, *, tt):
    t = pl.program_id(1)

    strips = _strips(u_ref.shape[1])

    @pl.when(t == 0)
    def _():
        for s, lanes in enumerate(strips):
            ext_ref[s, 0:CONV_HALO, :] = pre_ref[:, lanes]

    @pl.when(t > 0)
    def _():
        for s, lanes in enumerate(strips):
            ext_ref[s, 0:CONV_HALO, :] = halo_ref[:, lanes]

    for s, lanes in enumerate(strips):
        ext_ref[s, CONV_HALO:, :] = u_ref[:, lanes]
    _conv_taps(ext_ref, 0, cw_ref, cb_ref, y_ref, tt)
    o_ref[...] = _ln_swish(y_ref[...], lg_ref[...], lb_ref[...]).astype(o_ref.dtype)


def _conv_module(u, prefix, conv_w, conv_b, ln_g, ln_b, layer, *, seq_len, tt_pref=256):
    n, c = u.shape
    b = n // seq_len
    tt = _row_tile(seq_len, tt_pref)
    assert tt % CONV_HALO == 0
    nt = seq_len // tt
    hb = tt // CONV_HALO
    kern = functools.partial(_conv_kernel, tt=tt)
    vec = pl.BlockSpec((None, 1, c), lambda i, t: (layer, 0, 0))
    return pl.pallas_call(
        kern,
        grid=(b, nt),
        in_specs=[
            pl.BlockSpec((tt, c), lambda i, t: (i * nt + t, 0)),
            pl.BlockSpec((CONV_HALO, c), lambda i, t: (jnp.maximum((i * nt + t) * hb - 1, 0), 0)),
            pl.BlockSpec((None, CONV_HALO, c), lambda i, t: (i, 0, 0)),
            pl.BlockSpec((None, CONV_WIDTH, SUBLANES, c), lambda i, t: (layer, 0, 0, 0)),
            pl.BlockSpec((None, SUBLANES, c), lambda i, t: (layer, 0, 0)),
            vec, vec,
        ],
        out_specs=pl.BlockSpec((tt, c), lambda i, t: (i * nt + t, 0)),
        out_shape=jax.ShapeDtypeStruct((n, c), BF16),
        scratch_shapes=[pltpu.VMEM((c // LANES, tt + CONV_HALO, LANES), F32), pltpu.VMEM((tt, c), F32)],
        compiler_params=_cparams(2),
        name="conv_module",
    )(u, u, prefix, conv_w, conv_b, ln_g, ln_b)


def _pair_scores(q2, kb):
    lane = lax.broadcasted_iota(jnp.int32, (CHUNK, PAIR), 1)
    lo = lane < HEAD_DIM
    zero = jnp.zeros_like(q2)
    qs = jnp.concatenate([jnp.where(lo, q2, zero), jnp.where(lo, zero, q2)], axis=0)
    return lax.dot_general(qs, kb, (((1,), (1,)), ((), ())), preferred_element_type=F32)


def _pair_softmax_pv(s, bias, vb, n_invalid):
    s = s + bias
    if n_invalid is None:
        first = s[:, :LANES]
        col = lax.broadcasted_iota(jnp.int32, first.shape, 1)
        s = jnp.concatenate([jnp.where(col >= CHUNK, first, NEG), s[:, LANES:]], axis=1)
    else:
        col = lax.broadcasted_iota(jnp.int32, s.shape, 1)
        s = jnp.where(col >= n_invalid, s, NEG)
    m = jnp.max(s, axis=-1, keepdims=True)
    e = jnp.exp2(s - m)
    l = jnp.sum(e, axis=-1, keepdims=True)
    r = jnp.dot(e.astype(BF16), vb, preferred_element_type=F32)
    r = r * (1.0 / l)
    lane = lax.broadcasted_iota(jnp.int32, (CHUNK, PAIR), 1)
    return jnp.where(lane < HEAD_DIM, r[:CHUNK], r[CHUNK:])


def _attn_prompt_kernel(q_ref, k_ref, v_ref, bias_ref, o_ref, kp_ref, vp_ref, s_ref, *, seq_len, npairs):
    pad = BAND_PAD - CHUNK
    kp_ref[0:pad, :] = jnp.zeros((pad, kp_ref.shape[1]), BF16)
    vp_ref[0:pad, :] = jnp.zeros((pad, vp_ref.shape[1]), BF16)
    kp_ref[pad:, :] = k_ref[...]
    vp_ref[pad:, :] = v_ref[...]
    n_chunks = seq_len // CHUNK
    n_masked = min(PAST_CHUNKS, n_chunks)
    assert n_chunks % 2 == 0 and n_masked % 2 == 0

    def scores(c, slot):
        r0 = pl.multiple_of(c * CHUNK, CHUNK)
        for hp in range(npairs):
            cols = slice(hp * PAIR, (hp + 1) * PAIR)
            s_ref[slot, hp] = _pair_scores(q_ref[pl.ds(r0, CHUNK), cols], kp_ref[pl.ds(r0, BAND_PAD), cols])

    def finish(c, slot, masked):
        r0 = pl.multiple_of(c * CHUNK, CHUNK)
        n_invalid = (PAST_CHUNKS + 1 - c) * CHUNK if masked else None
        for hp in range(npairs):
            cols = slice(hp * PAIR, (hp + 1) * PAIR)
            o2 = _pair_softmax_pv(s_ref[slot, hp], bias_ref[hp], vp_ref[pl.ds(r0, BAND_PAD), cols], n_invalid)
            o_ref[pl.ds(r0, CHUNK), cols] = o2.astype(o_ref.dtype)

    def two_chunks(cc, masked):
        c = 2 * cc
        scores(c + 1, 1)
        finish(c, 0, masked)
        scores(jnp.minimum(c + 2, n_chunks - 1), 0)
        finish(c + 1, 1, masked)

    def masked_body(cc, carry):
        two_chunks(cc, True)
        return carry

    def plain_body(cc, carry):
        two_chunks(cc, False)
        return carry

    scores(0, 0)
    lax.fori_loop(0, n_masked // 2, masked_body, 0)
    lax.fori_loop(n_masked // 2, n_chunks // 2, plain_body, 0)


def _attn_prompt(qkv, bias, layer, *, seq_len, group_pairs=2):
    n, da = qkv.shape[0], qkv.shape[1] // 3
    b = n // seq_len
    gp = min(group_pairs, da // PAIR)
    gw = gp * PAIR
    assert da % gw == 0
    ng = da // gw
    kern = functools.partial(_attn_prompt_kernel, seq_len=seq_len, npairs=gp)
    part = lambda k: pl.BlockSpec((seq_len, gw), lambda i, g: (i, k * ng + g))
    return pl.pallas_call(
        kern,
        grid=(b, ng),
        in_specs=[part(0), part(1), part(2),
                  pl.BlockSpec((None, gp, PAIR, BAND_PAD), lambda i, g: (layer, g, 0, 0))],
        out_specs=part(0),
        out_shape=jax.ShapeDtypeStruct((n, da), BF16),
        scratch_shapes=[pltpu.VMEM((seq_len + BAND_PAD - CHUNK, gw), BF16),
                        pltpu.VMEM((seq_len + BAND_PAD - CHUNK, gw), BF16),
                        pltpu.VMEM((2, gp, PAIR, BAND_PAD), F32)],
        compiler_params=_cparams(2),
        name="attn_prompt",
    )(qkv, qkv, qkv, bias)


def _attn_sample_kernel(q_ref, k_ref, v_ref, kt_ref, vt_ref, ck_ref, cv_ref, bias_ref,
                        o_ref, nk_ref, nv_ref, kb_ref, vb_ref, *, npairs):
    keep = ck_ref.shape[0]
    da = ck_ref.shape[1]
    kb_ref[0:CHUNK, :] = jnp.zeros((CHUNK, da), BF16)
    vb_ref[0:CHUNK, :] = jnp.zeros((CHUNK, da), BF16)
    kb_ref[CHUNK:CHUNK + keep, :] = ck_ref[...].astype(BF16)
    vb_ref[CHUNK:CHUNK + keep, :] = cv_ref[...].astype(BF16)
    kb_ref[CHUNK + keep:, :] = k_ref[...]
    vb_ref[CHUNK + keep:, :] = v_ref[...]
    nk_ref[0:keep - CHUNK, :] = ck_ref[CHUNK:, :]
    nv_ref[0:keep - CHUNK, :] = cv_ref[CHUNK:, :]
    nk_ref[keep - CHUNK:, :] = kt_ref[...]
    nv_ref[keep - CHUNK:, :] = vt_ref[...]
    cols = [slice(hp * PAIR, (hp + 1) * PAIR) for hp in range(npairs)]
    ss = [_pair_scores(q_ref[:, c], kb_ref[:, c]) for c in cols]
    for hp, c in enumerate(cols):
        o_ref[:, c] = _pair_softmax_pv(ss[hp], bias_ref[hp], vb_ref[:, c], None).astype(o_ref.dtype)


def _attn_sample(qkv, kt, vt, cache_k, cache_v, bias, layer):
    n, da = qkv.shape[0], qkv.shape[1] // 3
    b = n // CHUNK
    keep = cache_k.shape[2]
    assert keep == BAND_PAST
    npairs = da // PAIR
    kern = functools.partial(_attn_sample_kernel, npairs=npairs)
    part = lambda k: pl.BlockSpec((CHUNK, da), lambda i: (i, k))
    cache = pl.BlockSpec((None, None, keep, da), lambda i: (layer, i, 0, 0))
    newc = pl.BlockSpec((None, keep, da), lambda i: (i, 0, 0))
    return pl.pallas_call(
        kern,
        grid=(b,),
        in_specs=[part(0), part(1), part(2), part(0), part(0), cache, cache,
                  pl.BlockSpec((None, npairs, PAIR, BAND_PAD), lambda i: (layer, 0, 0, 0))],
        out_specs=[part(0), newc, newc],
        out_shape=[jax.ShapeDtypeStruct((n, da), BF16),
                   jax.ShapeDtypeStruct((b, keep, da), F32),
                   jax.ShapeDtypeStruct((b, keep, da), F32)],
        scratch_shapes=[pltpu.VMEM((BAND_PAD, da), BF16), pltpu.VMEM((BAND_PAD, da), BF16)],
        compiler_params=_cparams(1),
        name="attn_sample",
    )(qkv, qkv, qkv, kt, vt, cache_k, cache_v, bias)


def _band_bias(rel_bias):
    depth, h, _ = rel_bias.shape
    n_far = BAND + CHUNK - 1 - REL_CLIP + 1
    n_near = BAND_PAD + CHUNK - 1 - n_far
    far = jnp.broadcast_to(rel_bias[:, :, 2 * REL_CLIP:], (depth, h, n_far))
    near = lax.rev(rel_bias[:, :, 2 * REL_CLIP - n_near:2 * REL_CLIP], (2,))
    g = jnp.concatenate([far, near], axis=2)
    g = g.astype(F32) * LOG2E
    rows = [g[:, :, CHUNK - 1 - i:CHUNK - 1 - i + BAND_PAD] for i in range(CHUNK)]
    return jnp.stack(rows, axis=2).reshape(depth, h // 2, PAIR, BAND_PAD)


def _gatemix_kernel(h_ref, c_ref, a_ref, wco_ref, wo_ref, wg1_ref, wg2_ref, b1_ref, b2_ref, m_ref):
    h = h_ref[...]
    gc = _sigmoid(jnp.dot(h, wg1_ref[...], preferred_element_type=F32) + b1_ref[...])
    yc = jnp.dot(c_ref[...], wco_ref[...], preferred_element_type=F32)
    mc = gc * yc
    ga = _sigmoid(jnp.dot(h, wg2_ref[...], preferred_element_type=F32) + b2_ref[...])
    ya = jnp.dot(a_ref[...], wo_ref[...], preferred_element_type=F32)
    m_ref[...] = (mc + ga * ya).astype(BF16)


def _gatemix(h, cb, ab, wco, wo, wg, bg, layer, *, tm_pref=1024, tn=512):
    n, d = h.shape
    dc, da = cb.shape[1], ab.shape[1]
    tm = _row_tile(n, tm_pref)
    assert d % tn == 0
    nc = d // tn
    return pl.pallas_call(
        _gatemix_kernel,
        grid=(n // tm, nc),
        in_specs=[
            pl.BlockSpec((tm, d), lambda i, p: (i, 0)),
            pl.BlockSpec((tm, dc), lambda i, p: (i, 0)),
            pl.BlockSpec((tm, da), lambda i, p: (i, 0)),
            pl.BlockSpec((None, dc, tn), lambda i, p: (layer, 0, p)),
            pl.BlockSpec((None, da, tn), lambda i, p: (layer, 0, p)),
            pl.BlockSpec((None, d, tn), lambda i, p: (layer, 0, p)),
            pl.BlockSpec((None, d, tn), lambda i, p: (layer, 0, nc + p)),
            pl.BlockSpec((None, 1, tn), lambda i, p: (layer, 0, p)),
            pl.BlockSpec((None, 1, tn), lambda i, p: (layer, 0, nc + p)),
        ],
        out_specs=pl.BlockSpec((tm, tn), lambda i, p: (i, p)),
        out_shape=jax.ShapeDtypeStruct((n, d), BF16),
        compiler_params=_cparams(2),
        name="gatemix",
    )(h, cb, ab, wco, wo, wg, wg, bg, bg)


def _ffn_kernel(x_ref, m_ref, wout_ref, g_ref, wg_ref, wu_ref, wdn_ref, gp_ref, o_ref, *rest, nf, final_norm):
    hn_ref, h_ref = (None, rest[0]) if final_norm else rest
    j = pl.program_id(1)

    @pl.when(j == 0)
    def _():
        x1 = x_ref[...] + jnp.dot(m_ref[...], wout_ref[...], preferred_element_type=F32)
        h_ref[...] = _rms_rows(x1, g_ref[...]).astype(BF16)
        o_ref[...] = x1

    h = h_ref[...]
    gate = jnp.dot(h, wg_ref[...], preferred_element_type=F32)
    up = jnp.dot(h, wu_ref[...], preferred_element_type=F32)
    act = ((gate * _sigmoid(gate)) * up).astype(BF16)
    o_ref[...] += jnp.dot(act, wdn_ref[...], preferred_element_type=F32)

    @pl.when(j == nf - 1)
    def _():
        y = _rms_rows(o_ref[...], gp_ref[...])
        if final_norm:
            o_ref[...] = y
        else:
            hn_ref[...] = y.astype(BF16)


def _ffn(x, m, wout, g, wup, wdn, g_post, post_idx, layer, *, final_norm, tm_pref=512, tf=512):
    n, d = x.shape
    dff = wdn.shape[1]
    tm = _row_tile(n, tm_pref)
    assert dff % tf == 0
    nf = dff // tf
    kern = functools.partial(_ffn_kernel, nf=nf, final_norm=final_norm)
    rows = pl.BlockSpec((tm, d), lambda i, j: (i, 0))
    out_specs, out_shape = [rows], [jax.ShapeDtypeStruct((n, d), F32)]
    if not final_norm:
        out_specs.append(rows)
        out_shape.append(jax.ShapeDtypeStruct((n, d), BF16))
    return pl.pallas_call(
        kern,
        grid=(n // tm, nf),
        in_specs=[
            rows,
            rows,
            pl.BlockSpec((None, d, d), lambda i, j: (layer, 0, 0), pipeline_mode=pl.Buffered(1)),
            pl.BlockSpec((None, 1, d), lambda i, j: (layer, 0, 0)),
            pl.BlockSpec((None, d, tf), lambda i, j: (layer, 0, j)),
            pl.BlockSpec((None, d, tf), lambda i, j: (layer, 0, nf + j)),
            pl.BlockSpec((None, tf, d), lambda i, j: (layer, j, 0)),
            pl.BlockSpec((None, 1, d), lambda i, j: (post_idx, 0, 0)),
        ],
        out_specs=out_specs,
        out_shape=out_shape,
        scratch_shapes=[pltpu.VMEM((tm, d), BF16)],
        compiler_params=_cparams(2),
        name="ffn",
    )(x, m, wout, g, wup, wup, wdn, g_post)


def kernel(x_prompt, x_sample, cache_k, cache_v, state_conv, norm_mix, w_in, conv_w, conv_b, ln_g, ln_b,
           w_conv_out, rel_bias, w_o, w_gate, b_gate, w_out, norm_ffn, w_up, w_down, norm_final):
    depth = w_in.shape[0]
    bp, tp, d = x_prompt.shape
    bs, ts, _ = x_sample.shape
    dc = conv_w.shape[2]
    da = w_o.shape[1]
    dff = w_down.shape[1]
    keep_s = cache_k.shape[2]
    nh = da // HEAD_DIM
    n_state = CONV_WIDTH - 1
    assert ts == CHUNK and tp % CHUNK == 0
    tf = 512 if dff % 512 == 0 else 256

    w_ab_b, w_qkv_b = w_in[:, :, :2 * dc].astype(BF16), w_in[:, :, 2 * dc:].astype(BF16)
    w_up_b, w_down_b, w_co_b, w_o_b, w_gate_b, w_out_b = (
        w.astype(BF16) for w in (w_up, w_down, w_conv_out, w_o, w_gate, w_out))
    row3 = lambda a: a.reshape(depth, 1, a.shape[-1])
    g_mix, g_ffn, cb3, lg3, lb3, bg3 = map(row3, (norm_mix, norm_ffn, conv_b, ln_g, ln_b, b_gate))
    g_fin = norm_final.reshape(1, 1, d)
    bias = _band_bias(rel_bias)
    ck = cache_k.reshape(depth, bs, keep_s, da)
    cv = cache_v.reshape(depth, bs, keep_s, da)
    conv_p = (jnp.broadcast_to(conv_w[:, :, None, :], (depth, CONV_WIDTH, SUBLANES, dc)),
              jnp.broadcast_to(cb3, (depth, SUBLANES, dc)), lg3, lb3)

    xp = x_prompt.reshape(bp * tp, d)
    xs = x_sample.reshape(bs * ts, d)
    keep_p = min(BAND_PAST, tp)
    heads = lambda a, b, keep: a.reshape(b, keep, nh, HEAD_DIM)
    kp, vp, cp, ksl, vsl, csl = [], [], [], [], [], []
    qkv_scale = jnp.where(jnp.arange(3 * da) < da, QK_SCALE, 1.0).astype(F32).reshape(1, 3 * da)
    hp = hs = None
    for l in range(depth):
        first, last = l == 0, l == depth - 1
        in_w = (w_ab_b, w_qkv_b, qkv_scale, l)
        mix_w = (w_co_b, w_o_b, w_gate_b, bg3, l)
        ffn_w = (w_out_b, g_ffn, w_up_b, w_down_b) + ((g_fin, 0, l) if last else (g_mix, l + 1, l))
        if first:
            hp, cact, qkv, kt, vt, ust = _inproj(xp, g_mix, *in_w, seq_len=tp, conv=conv_p)
        else:
            cact, qkv, kt, vt, ust = _inproj(hp, None, *in_w, seq_len=tp, conv=conv_p, from_x=False, tm_pref=1024)
        att = _attn_prompt(qkv, bias, l, seq_len=tp)
        out = _ffn(xp, _gatemix(hp, cact, att, *mix_w), *ffn_w, final_norm=last, tf=tf)
        xp, hp = (out[0], None) if last else out
        kp.append(heads(kt, bp, keep_p))
        vp.append(heads(vt, bp, keep_p))
        cp.append(ust[:, CONV_HALO - n_state:])
        if first:
            hs, u, qkv, kt, vt = _inproj(xs, g_mix, *in_w, seq_len=ts)
        else:
            u, qkv, kt, vt = _inproj(hs, None, *in_w, seq_len=ts, from_x=False)
        att, nk, nv = _attn_sample(qkv, kt, vt, ck, cv, bias, l)
        pre_s = jnp.pad(state_conv[l], ((0, 0), (CONV_HALO - n_state, 0), (0, 0)))
        cact = _conv_module(u, pre_s, *conv_p, l, seq_len=ts)
        out = _ffn(xs, _gatemix(hs, cact, att, *mix_w), *ffn_w, final_norm=last, tf=tf)
        xs, hs = (out[0], None) if last else out
        ksl.append(heads(nk, bs, keep_s))
        vsl.append(heads(nv, bs, keep_s))
        csl.append(u.reshape(bs, ts, dc)[:, ts - n_state:])
    return (xp.reshape(bp, tp, d), xs.reshape(bs, ts, d), jnp.stack(kp), jnp.stack(vp), jnp.stack(cp),
            jnp.stack(ksl), jnp.stack(vsl), jnp.stack(csl))
```

```python
import functools

import jax
import jax.numpy as jnp
from jax import lax
from jax.experimental import pallas as pl
from jax.experimental.pallas import tpu as pltpu

F32 = jnp.float32
BF16 = jnp.bfloat16

CHUNK = 64
PAST_CHUNKS = 8
BAND_PAST = PAST_CHUNKS * CHUNK
BAND = BAND_PAST + CHUNK
BAND_PAD = BAND + CHUNK
HEAD_DIM = 64
CONV_WIDTH = 31
CONV_HALO = 32
CONV_ROWS = 32
REL_CLIP = 128
EPS = 1e-6
NEG = -1e30
LOG2E = 1.4426950408889634
QK_SCALE = HEAD_DIM ** -0.5 * LOG2E

LANES = 128
SUBLANES = 8
PAIR = 2 * HEAD_DIM
VMEM_LIMIT_BYTES = 56 * 1024 * 1024
N_QKV_STEPS = 4


def _cparams(ndims):
    return pltpu.CompilerParams(dimension_semantics=("arbitrary",) * ndims,
                                vmem_limit_bytes=VMEM_LIMIT_BYTES)


def _sigmoid(x):
    return 1.0 / (1.0 + jnp.exp(-x))


def _rms_rows(x, g):
    ms = jnp.mean(x * x, axis=-1, keepdims=True)
    return (x * lax.rsqrt(ms + EPS)) * g


def _row_tile(n, pref):
    t = min(n, pref)
    assert n % t == 0, (n, t)
    return t


def _strips(c):
    assert c % LANES == 0
    return [slice(s * LANES, (s + 1) * LANES) for s in range(c // LANES)]


def _conv_taps(uext_ref, base, cw_ref, cb_ref, y_ref, rows):
    first = CONV_HALO - (CONV_WIDTH - 1)
    group = min(rows, CONV_ROWS)
    assert rows % group == 0 and group % SUBLANES == 0
    prev = None
    for s, lanes in enumerate(_strips(y_ref.shape[1])):
        for r in range(0, rows, group):
            init = cb_ref[:, lanes]
            if prev is not None:
                zero = lax.shift_right_logical(lax.shift_right_logical(prev, jnp.uint32(16)), jnp.uint32(16))
                init = lax.bitcast_convert_type(lax.bitcast_convert_type(init, jnp.uint32) | zero, F32)
            accs = [init] * (group // SUBLANES)
            for w in range(CONV_WIDTH):
                wt = cw_ref[w, :, lanes]
                accs = [a + uext_ref[s, pl.ds(base + r + t * SUBLANES + first + w, SUBLANES), :] * wt
                        for t, a in enumerate(accs)]
            for t, a in enumerate(accs):
                y_ref[r + t * SUBLANES:r + (t + 1) * SUBLANES, lanes] = a
            prev = functools.reduce(jnp.bitwise_or, [lax.bitcast_convert_type(a, jnp.uint32) for a in accs])


def _ln_swish(y, lg, lb):
    mu = jnp.mean(y, axis=-1, keepdims=True)
    d = y - mu
    var = jnp.mean(d * d, axis=-1, keepdims=True)
    z = (d * lax.rsqrt(var + EPS)) * lg + lb
    return z * _sigmoid(z)


def _kv_pieces(t, tq, da):
    lo, hi = t * tq, (t + 1) * tq
    out = []
    for part in (0, 1):
        plo, phi = (1 + part) * da, (2 + part) * da
        a, b = max(lo, plo), min(hi, phi)
        if a < b:
            out.append((part, a - plo, b - plo, a - lo, b - lo))
    return out


def _inproj_kernel(*refs, nu, nq, tm, tu, tq, da, tail_rows, tail_pred, bps, fuse_conv, from_x):
    refs = list(refs)
    take = lambda k: [refs.pop(0) for _ in range(k)]
    if from_x:
        x_ref, g_ref = take(2)
    else:
        (h_ref,) = take(1)
    wa_ref, wb_ref, wq_ref, sc_ref = take(4)
    if fuse_conv:
        cw_ref, cb_ref, lg_ref, lb_ref = take(4)
    if from_x:
        (h_ref,) = take(1)
    cu_ref, qkv_ref, kt_ref, vt_ref = take(4)
    if fuse_conv:
        st_ref, uext_ref, y_ref = take(3)
    else:
        (uext_ref,) = take(1)
    i = pl.program_id(0)
    j = pl.program_id(1)
    ib = i % bps

    @pl.when(j == 0)
    def _():
        if from_x:
            h_ref[...] = _rms_rows(x_ref[...], g_ref[...]).astype(BF16)
        if fuse_conv:
            @pl.when(ib == 0)
            def _():
                uext_ref[:, 0:CONV_HALO, :] = jnp.zeros((uext_ref.shape[0], CONV_HALO, LANES), F32)

            @pl.when(ib != 0)
            def _():
                uext_ref[:, 0:CONV_HALO, :] = uext_ref[:, tm:tm + CONV_HALO, :]

    for jj in range(nu):
        @pl.when(j == jj)
        def _():
            h = h_ref[...]
            a = jnp.dot(h, wa_ref[...], preferred_element_type=F32)
            b = jnp.dot(h, wb_ref[...], preferred_element_type=F32)
            u = a * _sigmoid(b)
            for s, lanes in enumerate(_strips(tu)):
                uext_ref[jj * (tu // LANES) + s, CONV_HALO:, :] = u[:, lanes]

    is_tail = tail_pred(i)

    @pl.when(j >= nu)
    def _():
        p = jnp.dot(h_ref[...], wq_ref[...], preferred_element_type=F32)
        qkv_ref[...] = (p * sc_ref[...]).astype(BF16)
        if fuse_conv:
            rows = tm // nq
            r0 = pl.multiple_of((j - nu) * rows, rows)
            _conv_taps(uext_ref, r0, cw_ref, cb_ref, y_ref, rows)
            cu_ref[pl.ds(r0, rows), :] = _ln_swish(y_ref[...], lg_ref[...], lb_ref[...]).astype(BF16)

        for t in range(nq):
            pieces = _kv_pieces(t, tq, da)
            if pieces:
                @pl.when(is_tail & (j == nu + t))
                def _(pieces=pieces):
                    for part, dlo, dhi, slo, shi in pieces:
                        (vt_ref if part else kt_ref)[:, dlo:dhi] = p[tm - tail_rows:, slo:shi]

        if fuse_conv:
            @pl.when((j == nu) & (ib == bps - 1))
            def _():
                for s, lanes in enumerate(_strips(st_ref.shape[1])):
                    st_ref[:, lanes] = uext_ref[s, tm:tm + CONV_HALO, :]
        else:
            @pl.when(j == nu)
            def _():
                for s, lanes in enumerate(_strips(cu_ref.shape[1])):
                    cu_ref[:, lanes] = uext_ref[s, CONV_HALO:, :]


def _inproj(src, g, w_ab, w_qkv, qkv_scale, layer, *, seq_len, conv=None, from_x=True, tm_pref=512):
    n, d = src.shape
    dc, dqkv = w_ab.shape[2] // 2, w_qkv.shape[2]
    da = dqkv // 3
    tm = _row_tile(n, tm_pref)
    nu, nq = 2, N_QKV_STEPS
    tu, tq = dc // nu, dqkv // nq
    assert tu % LANES == 0 and tq % LANES == 0
    ni, nj = n // tm, nu + nq
    keep = min(BAND_PAST, seq_len)
    fuse_conv = conv is not None
    if seq_len % tm == 0:
        bps = seq_len // tm
        assert tm % keep == 0 or keep % tm == 0
        tail_rows = min(tm, keep)
        ntb = keep // tail_rows
        n_tail = (n // seq_len) * keep

        def tail_blk(i):
            return (i // bps) * ntb + jnp.clip(i % bps - (bps - ntb), 0, ntb - 1)

        def tail_pred(i):
            return (i % bps) >= (bps - ntb)
    else:
        assert tm % seq_len == 0 and keep == seq_len and not fuse_conv
        bps, tail_rows, n_tail = 1, tm, n

        def tail_blk(i):
            return i

        def tail_pred(i):
            return i >= 0

    kern = functools.partial(_inproj_kernel, nu=nu, nq=nq, tm=tm, tu=tu, tq=tq, da=da, tail_rows=tail_rows,
                             tail_pred=tail_pred, bps=bps, fuse_conv=fuse_conv, from_x=from_x)
    ucol = lambda j: jnp.minimum(j, nu - 1)
    qcol = lambda j: jnp.clip(j - nu, 0, nq - 1)
    vec = pl.BlockSpec((None, 1, dc), lambda i, j: (layer, 0, 0))
    rows = pl.BlockSpec((tm, d), lambda i, j: (i, 0))
    in_specs = [rows, pl.BlockSpec((None, 1, d), lambda i, j: (layer, 0, 0))] if from_x else [rows]
    args = [src, g] if from_x else [src]
    in_specs += [
        pl.BlockSpec((None, d, tu), lambda i, j: (layer, 0, ucol(j))),
        pl.BlockSpec((None, d, tu), lambda i, j: (layer, 0, nu + ucol(j))),
        pl.BlockSpec((None, d, tq), lambda i, j: (layer, 0, qcol(j))),
        pl.BlockSpec((1, tq), lambda i, j: (0, qcol(j))),
    ]
    args += [w_ab, w_ab, w_qkv, qkv_scale]
    tail = pl.BlockSpec((tail_rows, da), lambda i, j: (tail_blk(i), 0))
    out_specs = ([rows] if from_x else []) + [
        pl.BlockSpec((tm, dc), lambda i, j: (i, 0)),
        pl.BlockSpec((tm, tq), lambda i, j: (i, qcol(j))),
        tail, tail,
    ]
    out_shape = ([jax.ShapeDtypeStruct((n, d), BF16)] if from_x else []) + [
        jax.ShapeDtypeStruct((n, dc), BF16 if fuse_conv else F32),
        jax.ShapeDtypeStruct((n, dqkv), BF16),
        jax.ShapeDtypeStruct((n_tail, da), F32),
        jax.ShapeDtypeStruct((n_tail, da), F32),
    ]
    scratch = [pltpu.VMEM((dc // LANES, tm + CONV_HALO, LANES), F32)]
    if fuse_conv:
        in_specs += [pl.BlockSpec((None, CONV_WIDTH, SUBLANES, dc), lambda i, j: (layer, 0, 0, 0)),
                     pl.BlockSpec((None, SUBLANES, dc), lambda i, j: (layer, 0, 0)), vec, vec]
        args += list(conv)
        out_specs.append(pl.BlockSpec((None, CONV_HALO, dc), lambda i, j: (i // bps, 0, 0)))
        out_shape.append(jax.ShapeDtypeStruct((n // seq_len, CONV_HALO, dc), F32))
        scratch.append(pltpu.VMEM((tm // nq, dc), F32))
    return pl.pallas_call(
        kern,
        grid=(ni, nj),
        in_specs=in_specs,
        out_specs=out_specs,
        out_shape=out_shape,
        scratch_shapes=scratch,
        compiler_params=_cparams(2),
        name="inproj_conv" if fuse_conv else "inproj",
    )(*args)


def _conv_kernel(u_ref, halo_ref, pre_ref, cw_ref, cb_ref, lg_ref, lb_ref, o_ref, ext_ref, y_ref, *, tt):
    t = pl.program_id(1)

    strips = _strips(u_ref.shape[1])

    @pl.when(t == 0)
    def _():
        for s, lanes in enumerate(strips):
            ext_ref[s, 0:CONV_HALO, :] = pre_ref[:, lanes]

    @pl.when(t > 0)
    def _():
        for s, lanes in enumerate(strips):
            ext_ref[s, 0:CONV_HALO, :] = halo_ref[:, lanes]

    for s, lanes in enumerate(strips):
        ext_ref[s, CONV_HALO:, :] = u_ref[:, lanes]
    _conv_taps(ext_ref, 0, cw_ref, cb_ref, y_ref, tt)
    o_ref[...] = _ln_swish(y_ref[...], lg_ref[...], lb_ref[...]).astype(o_ref.dtype)


def _conv_module(u, prefix, conv_w, conv_b, ln_g, ln_b, layer, *, seq_len, tt_pref=256):
    n, c = u.shape
    b = n // seq_len
    tt = _row_tile(seq_len, tt_pref)
    assert tt % CONV_HALO == 0
    nt = seq_len // tt
    hb = tt // CONV_HALO
    kern = functools.partial(_conv_kernel, tt=tt)
    vec = pl.BlockSpec((None, 1, c), lambda i, t: (layer, 0, 0))
    return pl.pallas_call(
        kern,
        grid=(b, nt),
        in_specs=[
            pl.BlockSpec((tt, c), lambda i, t: (i * nt + t, 0)),
            pl.BlockSpec((CONV_HALO, c), lambda i, t: (jnp.maximum((i * nt + t) * hb - 1, 0), 0)),
            pl.BlockSpec((None, CONV_HALO, c), lambda i, t: (i, 0, 0)),
            pl.BlockSpec((None, CONV_WIDTH, SUBLANES, c), lambda i, t: (layer, 0, 0, 0)),
            pl.BlockSpec((None, SUBLANES, c), lambda i, t: (layer, 0, 0)),
            vec, vec,
        ],
        out_specs=pl.BlockSpec((tt, c), lambda i, t: (i * nt + t, 0)),
        out_shape=jax.ShapeDtypeStruct((n, c), BF16),
        scratch_shapes=[pltpu.VMEM((c // LANES, tt + CONV_HALO, LANES), F32), pltpu.VMEM((tt, c), F32)],
        compiler_params=_cparams(2),
        name="conv_module",
    )(u, u, prefix, conv_w, conv_b, ln_g, ln_b)


def _pair_scores(q2, kb):
    lane = lax.broadcasted_iota(jnp.int32, (CHUNK, PAIR), 1)
    lo = lane < HEAD_DIM
    zero = jnp.zeros_like(q2)
    qs = jnp.concatenate([jnp.where(lo, q2, zero), jnp.where(lo, zero, q2)], axis=0)
    return lax.dot_general(qs, kb, (((1,), (1,)), ((), ())), preferred_element_type=F32)


def _pair_softmax_pv(s, bias, vb, n_invalid):
    s = s + bias
    if n_invalid is None:
        first = s[:, :LANES]
        col = lax.broadcasted_iota(jnp.int32, first.shape, 1)
        s = jnp.concatenate([jnp.where(col >= CHUNK, first, NEG), s[:, LANES:]], axis=1)
    else:
        col = lax.broadcasted_iota(jnp.int32, s.shape, 1)
        s = jnp.where(col >= n_invalid, s, NEG)
    m = jnp.max(s, axis=-1, keepdims=True)
    e = jnp.exp2(s - m)
    l = jnp.sum(e, axis=-1, keepdims=True)
    r = jnp.dot(e.astype(BF16), vb, preferred_element_type=F32)
    r = r * (1.0 / l)
    lane = lax.broadcasted_iota(jnp.int32, (CHUNK, PAIR), 1)
    return jnp.where(lane < HEAD_DIM, r[:CHUNK], r[CHUNK:])


def _attn_prompt_kernel(q_ref, k_ref, v_ref, bias_ref, o_ref, kp_ref, vp_ref, s_ref, *, seq_len, npairs):
    pad = BAND_PAD - CHUNK
    kp_ref[0:pad, :] = jnp.zeros((pad, kp_ref.shape[1]), BF16)
    vp_ref[0:pad, :] = jnp.zeros((pad, vp_ref.shape[1]), BF16)
    kp_ref[pad:, :] = k_ref[...]
    vp_ref[pad:, :] = v_ref[...]
    n_chunks = seq_len // CHUNK
    n_masked = min(PAST_CHUNKS, n_chunks)
    assert n_chunks % 2 == 0 and n_masked % 2 == 0

    def scores(c, slot):
        r0 = pl.multiple_of(c * CHUNK, CHUNK)
        for hp in range(npairs):
            cols = slice(hp * PAIR, (hp + 1) * PAIR)
            s_ref[slot, hp] = _pair_scores(q_ref[pl.ds(r0, CHUNK), cols], kp_ref[pl.ds(r0, BAND_PAD), cols])

    def finish(c, slot, masked):
        r0 = pl.multiple_of(c * CHUNK, CHUNK)
        n_invalid = (PAST_CHUNKS + 1 - c) * CHUNK if masked else None
        for hp in range(npairs):
            cols = slice(hp * PAIR, (hp + 1) * PAIR)
            o2 = _pair_softmax_pv(s_ref[slot, hp], bias_ref[hp], vp_ref[pl.ds(r0, BAND_PAD), cols], n_invalid)
            o_ref[pl.ds(r0, CHUNK), cols] = o2.astype(o_ref.dtype)

    def two_chunks(cc, masked):
        c = 2 * cc
        scores(c + 1, 1)
        finish(c, 0, masked)
        scores(jnp.minimum(c + 2, n_chunks - 1), 0)
        finish(c + 1, 1, masked)

    def masked_body(cc, carry):
        two_chunks(cc, True)
        return carry

    def plain_body(cc, carry):
        two_chunks(cc, False)
        return carry

    scores(0, 0)
    lax.fori_loop(0, n_masked // 2, masked_body, 0)
    lax.fori_loop(n_masked // 2, n_chunks // 2, plain_body, 0)


def _attn_prompt(qkv, bias, layer, *, seq_len, group_pairs=2):
    n, da = qkv.shape[0], qkv.shape[1] // 3
    b = n // seq_len
    gp = min(group_pairs, da // PAIR)
    gw = gp * PAIR
    assert da % gw == 0
    ng = da // gw
    kern = functools.partial(_attn_prompt_kernel, seq_len=seq_len, npairs=gp)
    part = lambda k: pl.BlockSpec((seq_len, gw), lambda i, g: (i, k * ng + g))
    return pl.pallas_call(
        kern,
        grid=(b, ng),
        in_specs=[part(0), part(1), part(2),
                  pl.BlockSpec((None, gp, PAIR, BAND_PAD), lambda i, g: (layer, g, 0, 0))],
        out_specs=part(0),
        out_shape=jax.ShapeDtypeStruct((n, da), BF16),
        scratch_shapes=[pltpu.VMEM((seq_len + BAND_PAD - CHUNK, gw), BF16),
                        pltpu.VMEM((seq_len + BAND_PAD - CHUNK, gw), BF16),
                        pltpu.VMEM((2, gp, PAIR, BAND_PAD), F32)],
        compiler_params=_cparams(2),
        name="attn_prompt",
    )(qkv, qkv, qkv, bias)


def _attn_sample_kernel(q_ref, k_ref, v_ref, kt_ref, vt_ref, ck_ref, cv_ref, bias_ref,
                        o_ref, nk_ref, nv_ref, kb_ref, vb_ref, *, npairs):
    keep = ck_ref.shape[0]
    da = ck_ref.shape[1]
    kb_ref[0:CHUNK, :] = jnp.zeros((CHUNK, da), BF16)
    vb_ref[0:CHUNK, :] = jnp.zeros((CHUNK, da), BF16)
    kb_ref[CHUNK:CHUNK + keep, :] = ck_ref[...].astype(BF16)
    vb_ref[CHUNK:CHUNK + keep, :] = cv_ref[...].astype(BF16)
    kb_ref[CHUNK + keep:, :] = k_ref[...]
    vb_ref[CHUNK + keep:, :] = v_ref[...]
    nk_ref[0:keep - CHUNK, :] = ck_ref[CHUNK:, :]
    nv_ref[0:keep - CHUNK, :] = cv_ref[CHUNK:, :]
    nk_ref[keep - CHUNK:, :] = kt_ref[...]
    nv_ref[keep - CHUNK:, :] = vt_ref[...]
    cols = [slice(hp * PAIR, (hp + 1) * PAIR) for hp in range(npairs)]
    ss = [_pair_scores(q_ref[:, c], kb_ref[:, c]) for c in cols]
    for hp, c in enumerate(cols):
        o_ref[:, c] = _pair_softmax_pv(ss[hp], bias_ref[hp], vb_ref[:, c], None).astype(o_ref.dtype)


def _attn_sample(qkv, kt, vt, cache_k, cache_v, bias, layer):
    n, da = qkv.shape[0], qkv.shape[1] // 3
    b = n // CHUNK
    keep = cache_k.shape[2]
    assert keep == BAND_PAST
    npairs = da // PAIR
    kern = functools.partial(_attn_sample_kernel, npairs=npairs)
    part = lambda k: pl.BlockSpec((CHUNK, da), lambda i: (i, k))
    cache = pl.BlockSpec((None, None, keep, da), lambda i: (layer, i, 0, 0))
    newc = pl.BlockSpec((None, keep, da), lambda i: (i, 0, 0))
    return pl.pallas_call(
        kern,
        grid=(b,),
        in_specs=[part(0), part(1), part(2), part(0), part(0), cache, cache,
                  pl.BlockSpec((None, npairs, PAIR, BAND_PAD), lambda i: (layer, 0, 0, 0))],
        out_specs=[part(0), newc, newc],
        out_shape=[jax.ShapeDtypeStruct((n, da), BF16),
                   jax.ShapeDtypeStruct((b, keep, da), F32),
                   jax.ShapeDtypeStruct((b, keep, da), F32)],
        scratch_shapes=[pltpu.VMEM((BAND_PAD, da), BF16), pltpu.VMEM((BAND_PAD, da), BF16)],
        compiler_params=_cparams(1),
        name="attn_sample",
    )(qkv, qkv, qkv, kt, vt, cache_k, cache_v, bias)


def _band_bias(rel_bias):
    depth, h, _ = rel_bias.shape
    n_far = BAND + CHUNK - 1 - REL_CLIP + 1
    n_near = BAND_PAD + CHUNK - 1 - n_far
    far = jnp.broadcast_to(rel_bias[:, :, 2 * REL_CLIP:], (depth, h, n_far))
    near = lax.rev(rel_bias[:, :, 2 * REL_CLIP - n_near:2 * REL_CLIP], (2,))
    g = jnp.concatenate([far, near], axis=2)
    g = g.astype(F32) * LOG2E
    rows = [g[:, :, CHUNK - 1 - i:CHUNK - 1 - i + BAND_PAD] for i in range(CHUNK)]
    return jnp.stack(rows, axis=2).reshape(depth, h // 2, PAIR, BAND_PAD)


def _gatemix_kernel(h_ref, c_ref, a_ref, wco_ref, wo_ref, wg1_ref, wg2_ref, b1_ref, b2_ref, m_ref):
    h = h_ref[...]
    gc = _sigmoid(jnp.dot(h, wg1_ref[...], preferred_element_type=F32) + b1_ref[...])
    yc = jnp.dot(c_ref[...], wco_ref[...], preferred_element_type=F32)
    mc = gc * yc
    ga = _sigmoid(jnp.dot(h, wg2_ref[...], preferred_element_type=F32) + b2_ref[...])
    ya = jnp.dot(a_ref[...], wo_ref[...], preferred_element_type=F32)
    m_ref[...] = (mc + ga * ya).astype(BF16)


def _gatemix(h, cb, ab, wco, wo, wg, bg, layer, *, tm_pref=1024, tn=512):
    n, d = h.shape
    dc, da = cb.shape[1], ab.shape[1]
    tm = _row_tile(n, tm_pref)
    assert d % tn == 0
    nc = d // tn
    return pl.pallas_call(
        _gatemix_kernel,
        grid=(n // tm, nc),
        in_specs=[
            pl.BlockSpec((tm, d), lambda i, p: (i, 0)),
            pl.BlockSpec((tm, dc), lambda i, p: (i, 0)),
            pl.BlockSpec((tm, da), lambda i, p: (i, 0)),
            pl.BlockSpec((None, dc, tn), lambda i, p: (layer, 0, p)),
            pl.BlockSpec((None, da, tn), lambda i, p: (layer, 0, p)),
            pl.BlockSpec((None, d, tn), lambda i, p: (layer, 0, p)),
            pl.BlockSpec((None, d, tn), lambda i, p: (layer, 0, nc + p)),
            pl.BlockSpec((None, 1, tn), lambda i, p: (layer, 0, p)),
            pl.BlockSpec((None, 1, tn), lambda i, p: (layer, 0, nc + p)),
        ],
        out_specs=pl.BlockSpec((tm, tn), lambda i, p: (i, p)),
        out_shape=jax.ShapeDtypeStruct((n, d), BF16),
        compiler_params=_cparams(2),
        name="gatemix",
    )(h, cb, ab, wco, wo, wg, wg, bg, bg)


def _ffn_kernel(x_ref, m_ref, wout_ref, g_ref, wg_ref, wu_ref, wdn_ref, gp_ref, o_ref, *rest, nf, final_norm):
    hn_ref, h_ref = (None, rest[0]) if final_norm else rest
    j = pl.program_id(1)

    @pl.when(j == 0)
    def _():
        x1 = x_ref[...] + jnp.dot(m_ref[...], wout_ref[...], preferred_element_type=F32)
        h_ref[...] = _rms_rows(x1, g_ref[...]).astype(BF16)
        o_ref[...] = x1

    h = h_ref[...]
    gate = jnp.dot(h, wg_ref[...], preferred_element_type=F32)
    up = jnp.dot(h, wu_ref[...], preferred_element_type=F32)
    act = ((gate * _sigmoid(gate)) * up).astype(BF16)
    o_ref[...] += jnp.dot(act, wdn_ref[...], preferred_element_type=F32)

    @pl.when(j == nf - 1)
    def _():
        y = _rms_rows(o_ref[...], gp_ref[...])
        if final_norm:
            o_ref[...] = y
        else:
            hn_ref[...] = y.astype(BF16)


def _ffn(x, m, wout, g, wup, wdn, g_post, post_idx, layer, *, final_norm, tm_pref=512, tf=512):
    n, d = x.shape
    dff = wdn.shape[1]
    tm = _row_tile(n, tm_pref)
    assert dff % tf == 0
    nf = dff // tf
    kern = functools.partial(_ffn_kernel, nf=nf, final_norm=final_norm)
    rows = pl.BlockSpec((tm, d), lambda i, j: (i, 0))
    out_specs, out_shape = [rows], [jax.ShapeDtypeStruct((n, d), F32)]
    if not final_norm:
        out_specs.append(rows)
        out_shape.append(jax.ShapeDtypeStruct((n, d), BF16))
    return pl.pallas_call(
        kern,
        grid=(n // tm, nf),
        in_specs=[
            rows,
            rows,
            pl.BlockSpec((None, d, d), lambda i, j: (layer, 0, 0), pipeline_mode=pl.Buffered(1)),
            pl.BlockSpec((None, 1, d), lambda i, j: (layer, 0, 0)),
            pl.BlockSpec((None, d, tf), lambda i, j: (layer, 0, j)),
            pl.BlockSpec((None, d, tf), lambda i, j: (layer, 0, nf + j)),
            pl.BlockSpec((None, tf, d), lambda i, j: (layer, j, 0)),
            pl.BlockSpec((None, 1, d), lambda i, j: (post_idx, 0, 0)),
        ],
        out_specs=out_specs,
        out_shape=out_shape,
        scratch_shapes=[pltpu.VMEM((tm, d), BF16)],
        compiler_params=_cparams(2),
        name="ffn",
    )(x, m, wout, g, wup, wup, wdn, g_post)


def kernel(x_prompt, x_sample, cache_k, cache_v, state_conv, norm_mix, w_in, conv_w, conv_b, ln_g, ln_b,
           w_conv_out, rel_bias, w_o, w_gate, b_gate, w_out, norm_ffn, w_up, w_down, norm_final):
    depth = w_in.shape[0]
    bp, tp, d = x_prompt.shape
    bs, ts, _ = x_sample.shape
    dc = conv_w.shape[2]
    da = w_o.shape[1]
    dff = w_down.shape[1]
    keep_s = cache_k.shape[2]
    nh = da // HEAD_DIM
    n_state = CONV_WIDTH - 1
    assert ts == CHUNK and tp % CHUNK == 0
    tf = 512 if dff % 512 == 0 else 256

    w_ab_b, w_qkv_b = w_in[:, :, :2 * dc].astype(BF16), w_in[:, :, 2 * dc:].astype(BF16)
    w_up_b, w_down_b, w_co_b, w_o_b, w_gate_b, w_out_b = (
        w.astype(BF16) for w in (w_up, w_down, w_conv_out, w_o, w_gate, w_out))
    row3 = lambda a: a.reshape(depth, 1, a.shape[-1])
    g_mix, g_ffn, cb3, lg3, lb3, bg3 = map(row3, (norm_mix, norm_ffn, conv_b, ln_g, ln_b, b_gate))
    g_fin = norm_final.reshape(1, 1, d)
    bias = _band_bias(rel_bias)
    ck = cache_k.reshape(depth, bs, keep_s, da)
    cv = cache_v.reshape(depth, bs, keep_s, da)
    conv_p = (jnp.broadcast_to(conv_w[:, :, None, :], (depth, CONV_WIDTH, SUBLANES, dc)),
              jnp.broadcast_to(cb3, (depth, SUBLANES, dc)), lg3, lb3)

    xp = x_prompt.reshape(bp * tp, d)
    xs = x_sample.reshape(bs * ts, d)
    keep_p = min(BAND_PAST, tp)
    heads = lambda a, b, keep: a.reshape(b, keep, nh, HEAD_DIM)
    kp, vp, cp, ksl, vsl, csl = [], [], [], [], [], []
    qkv_scale = jnp.where(jnp.arange(3 * da) < da, QK_SCALE, 1.0).astype(F32).reshape(1, 3 * da)
    hp = hs = None
    for l in range(depth):
        first, last = l == 0, l == depth - 1
        in_w = (w_ab_b, w_qkv_b, qkv_scale, l)
        mix_w = (w_co_b, w_o_b, w_gate_b, bg3, l)
        ffn_w = (w_out_b, g_ffn, w_up_b, w_down_b) + ((g_fin, 0, l) if last else (g_mix, l + 1, l))
        if first:
            hp, cact, qkv, kt, vt, ust = _inproj(xp, g_mix, *in_w, seq_len=tp, conv=conv_p)
        else:
            cact, qkv, kt, vt, ust = _inproj(hp, None, *in_w, seq_len=tp, conv=conv_p, from_x=False, tm_pref=1024)
        att = _attn_prompt(qkv, bias, l, seq_len=tp)
        out = _ffn(xp, _gatemix(hp, cact, att, *mix_w), *ffn_w, final_norm=last, tf=tf)
        xp, hp = (out[0], None) if last else out
        kp.append(heads(kt, bp, keep_p))
        vp.append(heads(vt, bp, keep_p))
        cp.append(ust[:, CONV_HALO - n_state:])
        if first:
            hs, u, qkv, kt, vt = _inproj(xs, g_mix, *in_w, seq_len=ts)
        else:
            u, qkv, kt, vt = _inproj(hs, None, *in_w, seq_len=ts, from_x=False)
        att, nk, nv = _attn_sample(qkv, kt, vt, ck, cv, bias, l)
        pre_s = jnp.pad(state_conv[l], ((0, 0), (CONV_HALO - n_state, 0), (0, 0)))
        cact = _conv_module(u, pre_s, *conv_p, l, seq_len=ts)
        out = _ffn(xs, _gatemix(hs, cact, att, *mix_w), *ffn_w, final_norm=last, tf=tf)
        xs, hs = (out[0], None) if last else out
        ksl.append(heads(nk, bs, keep_s))
        vsl.append(heads(nv, bs, keep_s))
        csl.append(u.reshape(bs, ts, dc)[:, ts - n_state:])
    return (xp.reshape(bp, tp, d), xs.reshape(bs, ts, d), jnp.stack(kp), jnp.stack(vp), jnp.stack(cp),
            jnp.stack(ksl), jnp.stack(vsl), jnp.stack(csl))
```

```python
import functools

import jax
import jax.numpy as jnp
from jax import lax
from jax.experimental import pallas as pl
from jax.experimental.pallas import tpu as pltpu

F32 = jnp.float32
BF16 = jnp.bfloat16

CHUNK = 64
PAST_CHUNKS = 8
BAND_PAST = PAST_CHUNKS * CHUNK
BAND = BAND_PAST + CHUNK
BAND_PAD = BAND + CHUNK
HEAD_DIM = 64
CONV_WIDTH = 31
CONV_HALO = 32
CONV_ROWS = 32
REL_CLIP = 128
EPS = 1e-6
NEG = -1e30
LOG2E = 1.4426950408889634
QK_SCALE = HEAD_DIM ** -0.5 * LOG2E

LANES = 128
SUBLANES = 8
PAIR = 2 * HEAD_DIM
VMEM_LIMIT_BYTES = 56 * 1024 * 1024
N_QKV_STEPS = 4


def _cparams(ndims):
    return pltpu.CompilerParams(dimension_semantics=("arbitrary",) * ndims,
                                vmem_limit_bytes=VMEM_LIMIT_BYTES)


def _sigmoid(x):
    return 1.0 / (1.0 + jnp.exp(-x))


def _rms_rows(x, g):
    ms = jnp.mean(x * x, axis=-1, keepdims=True)
    return (x * lax.rsqrt(ms + EPS)) * g


def _row_tile(n, pref):
    t = min(n, pref)
    assert n % t == 0, (n, t)
    return t


def _strips(c):
    assert c % LANES == 0
    return [slice(s * LANES, (s + 1) * LANES) for s in range(c // LANES)]


def _conv_taps(uext_ref, base, cw_ref, cb_ref, y_ref, rows):
    first = CONV_HALO - (CONV_WIDTH - 1)
    group = min(rows, CONV_ROWS)
    assert rows % group == 0 and group % SUBLANES == 0
    prev = None
    for s, lanes in enumerate(_strips(y_ref.shape[1])):
        for r in range(0, rows, group):
            init = cb_ref[:, lanes]
            if prev is not None:
                zero = lax.shift_right_logical(lax.shift_right_logical(prev, jnp.uint32(16)), jnp.uint32(16))
                init = lax.bitcast_convert_type(lax.bitcast_convert_type(init, jnp.uint32) | zero, F32)
            accs = [init] * (group // SUBLANES)
            for w in range(CONV_WIDTH):
                wt = cw_ref[w, :, lanes]
                accs = [a + uext_ref[s, pl.ds(base + r + t * SUBLANES + first + w, SUBLANES), :] * wt
                        for t, a in enumerate(accs)]
            for t, a in enumerate(accs):
                y_ref[r + t * SUBLANES:r + (t + 1) * SUBLANES, lanes] = a
            prev = functools.reduce(jnp.bitwise_or, [lax.bitcast_convert_type(a, jnp.uint32) for a in accs])


def _ln_swish(y, lg, lb):
    mu = jnp.mean(y, axis=-1, keepdims=True)
    d = y - mu
    var = jnp.mean(d * d, axis=-1, keepdims=True)
    z = (d * lax.rsqrt(var + EPS)) * lg + lb
    return z * _sigmoid(z)


def _first_norm_kernel(x_ref, g_ref, h_ref):
    h_ref[...] = _rms_rows(x_ref[...], g_ref[...]).astype(BF16)


def _first_norm(x, g, *, tm_pref=512):
    n, d = x.shape
    tm = _row_tile(n, tm_pref)
    rows = pl.BlockSpec((tm, d), lambda i: (i, 0))
    return pl.pallas_call(
        _first_norm_kernel,
        grid=(n // tm,),
        in_specs=[rows, pl.BlockSpec((None, 1, d), lambda i: (0, 0, 0))],
        out_specs=rows,
        out_shape=jax.ShapeDtypeStruct((n, d), BF16),
        compiler_params=_cparams(1),
        name="first_norm",
    )(x, g)


def _kv_pieces(t, tq, da):
    lo, hi = t * tq, (t + 1) * tq
    out = []
    for part in (0, 1):
        plo, phi = (1 + part) * da, (2 + part) * da
        a, b = max(lo, plo), min(hi, phi)
        if a < b:
            out.append((part, a - plo, b - plo, a - lo, b - lo))
    return out


def _inproj_kernel(*refs, nu, nq, tm, tu, tq, da, tail_rows, tail_pred, bps, fuse_conv):
    refs = list(refs)
    take = lambda k: [refs.pop(0) for _ in range(k)]
    h_ref, wa_ref, wb_ref, wq_ref, sc_ref = take(5)
    if fuse_conv:
        cw_ref, cb_ref, lg_ref, lb_ref = take(4)
    cu_ref, qkv_ref, kt_ref, vt_ref = take(4)
    if fuse_conv:
        st_ref, uext_ref, y_ref = take(3)
    else:
        (uext_ref,) = take(1)
    i = pl.program_id(0)
    j = pl.program_id(1)
    ib = i % bps

    if fuse_conv:
        @pl.when((j == 0) & (ib == 0))
        def _():
            uext_ref[:, 0:CONV_HALO, :] = jnp.zeros((uext_ref.shape[0], CONV_HALO, LANES), F32)

        @pl.when((j == 0) & (ib != 0))
        def _():
            uext_ref[:, 0:CONV_HALO, :] = uext_ref[:, tm:tm + CONV_HALO, :]

    for jj in range(nu):
        @pl.when(j == jj)
        def _():
            h = h_ref[...]
            a = jnp.dot(h, wa_ref[...], preferred_element_type=F32)
            b = jnp.dot(h, wb_ref[...], preferred_element_type=F32)
            u = a * _sigmoid(b)
            for s, lanes in enumerate(_strips(tu)):
                uext_ref[jj * (tu // LANES) + s, CONV_HALO:, :] = u[:, lanes]

    is_tail = tail_pred(i)

    @pl.when(j >= nu)
    def _():
        p = jnp.dot(h_ref[...], wq_ref[...], preferred_element_type=F32)
        qkv_ref[...] = (p * sc_ref[...]).astype(BF16)
        if fuse_conv:
            rows = tm // nq
            r0 = pl.multiple_of((j - nu) * rows, rows)
            _conv_taps(uext_ref, r0, cw_ref, cb_ref, y_ref, rows)
            cu_ref[pl.ds(r0, rows), :] = _ln_swish(y_ref[...], lg_ref[...], lb_ref[...]).astype(BF16)

        for t in range(nq):
            pieces = _kv_pieces(t, tq, da)
            if pieces:
                @pl.when(is_tail & (j == nu + t))
                def _(pieces=pieces):
                    for part, dlo, dhi, slo, shi in pieces:
                        (vt_ref if part else kt_ref)[:, dlo:dhi] = p[tm - tail_rows:, slo:shi]

        if fuse_conv:
            @pl.when((j == nu) & (ib == bps - 1))
            def _():
                for s, lanes in enumerate(_strips(st_ref.shape[1])):
                    st_ref[:, lanes] = uext_ref[s, tm:tm + CONV_HALO, :]
        else:
            @pl.when(j == nu)
            def _():
                for s, lanes in enumerate(_strips(cu_ref.shape[1])):
                    cu_ref[:, lanes] = uext_ref[s, CONV_HALO:, :]


def _inproj(h, w_ab, w_qkv, qkv_scale, layer, *, seq_len, conv=None, tm_pref=512):
    n, d = h.shape
    dc, dqkv = w_ab.shape[2] // 2, w_qkv.shape[2]
    da = dqkv // 3
    tm = _row_tile(n, tm_pref)
    nu, nq = 2, N_QKV_STEPS
    tu, tq = dc // nu, dqkv // nq
    assert tu % LANES == 0 and tq % LANES == 0
    ni, nj = n // tm, nu + nq
    keep = min(BAND_PAST, seq_len)
    fuse_conv = conv is not None
    if seq_len % tm == 0:
        bps = seq_len // tm
        assert tm % keep == 0 or keep % tm == 0
        tail_rows = min(tm, keep)
        ntb = keep // tail_rows
        n_tail = (n // seq_len) * keep

        def tail_blk(i):
            return (i // bps) * ntb + jnp.clip(i % bps - (bps - ntb), 0, ntb - 1)

        def tail_pred(i):
            return (i % bps) >= (bps - ntb)
    else:
        assert tm % seq_len == 0 and keep == seq_len and not fuse_conv
        bps, tail_rows, n_tail = 1, tm, n

        def tail_blk(i):
            return i

        def tail_pred(i):
            return i >= 0

    kern = functools.partial(_inproj_kernel, nu=nu, nq=nq, tm=tm, tu=tu, tq=tq, da=da, tail_rows=tail_rows,
                             tail_pred=tail_pred, bps=bps, fuse_conv=fuse_conv)
    ucol = lambda j: jnp.minimum(j, nu - 1)
    qcol = lambda j: jnp.clip(j - nu, 0, nq - 1)
    vec = pl.BlockSpec((None, 1, dc), lambda i, j: (layer, 0, 0))
    in_specs = [
        pl.BlockSpec((tm, d), lambda i, j: (i, 0)),
        pl.BlockSpec((None, d, tu), lambda i, j: (layer, 0, ucol(j))),
        pl.BlockSpec((None, d, tu), lambda i, j: (layer, 0, nu + ucol(j))),
        pl.BlockSpec((None, d, tq), lambda i, j: (layer, 0, qcol(j))),
        pl.BlockSpec((1, tq), lambda i, j: (0, qcol(j))),
    ]
    args = [h, w_ab, w_ab, w_qkv, qkv_scale]
    tail = pl.BlockSpec((tail_rows, da), lambda i, j: (tail_blk(i), 0))
    out_specs = [
        pl.BlockSpec((tm, dc), lambda i, j: (i, 0)),
        pl.BlockSpec((tm, tq), lambda i, j: (i, qcol(j))),
        tail, tail,
    ]
    out_shape = [
        jax.ShapeDtypeStruct((n, dc), BF16 if fuse_conv else F32),
        jax.ShapeDtypeStruct((n, dqkv), BF16),
        jax.ShapeDtypeStruct((n_tail, da), F32),
        jax.ShapeDtypeStruct((n_tail, da), F32),
    ]
    scratch = [pltpu.VMEM((dc // LANES, tm + CONV_HALO, LANES), F32)]
    if fuse_conv:
        in_specs += [pl.BlockSpec((None, CONV_WIDTH, SUBLANES, dc), lambda i, j: (layer, 0, 0, 0)),
                     pl.BlockSpec((None, SUBLANES, dc), lambda i, j: (layer, 0, 0)), vec, vec]
        args += list(conv)
        out_specs.append(pl.BlockSpec((None, CONV_HALO, dc), lambda i, j: (i // bps, 0, 0)))
        out_shape.append(jax.ShapeDtypeStruct((n // seq_len, CONV_HALO, dc), F32))
        scratch.append(pltpu.VMEM((tm // nq, dc), F32))
    return pl.pallas_call(
        kern,
        grid=(ni, nj),
        in_specs=in_specs,
        out_specs=out_specs,
        out_shape=out_shape,
        scratch_shapes=scratch,
        compiler_params=_cparams(2),
        name="inproj_conv" if fuse_conv else "inproj",
    )(*args)


def _conv_kernel(u_ref, halo_ref, pre_ref, cw_ref, cb_ref, lg_ref, lb_ref, o_ref, ext_ref, y_ref, *, tt):
    t = pl.program_id(1)

    strips = _strips(u_ref.shape[1])

    @pl.when(t == 0)
    def _():
        for s, lanes in enumerate(strips):
            ext_ref[s, 0:CONV_HALO, :] = pre_ref[:, lanes]

    @pl.when(t > 0)
    def _():
        for s, lanes in enumerate(strips):
            ext_ref[s, 0:CONV_HALO, :] = halo_ref[:, lanes]

    for s, lanes in enumerate(strips):
        ext_ref[s, CONV_HALO:, :] = u_ref[:, lanes]
    _conv_taps(ext_ref, 0, cw_ref, cb_ref, y_ref, tt)
    o_ref[...] = _ln_swish(y_ref[...], lg_ref[...], lb_ref[...]).astype(o_ref.dtype)


def _conv_module(u, prefix, conv_w, conv_b, ln_g, ln_b, layer, *, seq_len, tt_pref=256):
    n, c = u.shape
    b = n // seq_len
    tt = _row_tile(seq_len, tt_pref)
    assert tt % CONV_HALO == 0
    nt = seq_len // tt
    hb = tt // CONV_HALO
    kern = functools.partial(_conv_kernel, tt=tt)
    vec = pl.BlockSpec((None, 1, c), lambda i, t: (layer, 0, 0))
    return pl.pallas_call(
        kern,
        grid=(b, nt),
        in_specs=[
            pl.BlockSpec((tt, c), lambda i, t: (i * nt + t, 0)),
            pl.BlockSpec((CONV_HALO, c), lambda i, t: (jnp.maximum((i * nt + t) * hb - 1, 0), 0)),
            pl.BlockSpec((None, CONV_HALO, c), lambda i, t: (i, 0, 0)),
            pl.BlockSpec((None, CONV_WIDTH, SUBLANES, c), lambda i, t: (layer, 0, 0, 0)),
            pl.BlockSpec((None, SUBLANES, c), lambda i, t: (layer, 0, 0)),
            vec, vec,
        ],
        out_specs=pl.BlockSpec((tt, c), lambda i, t: (i * nt + t, 0)),
        out_shape=jax.ShapeDtypeStruct((n, c), BF16),
        scratch_shapes=[pltpu.VMEM((c // LANES, tt + CONV_HALO, LANES), F32), pltpu.VMEM((tt, c), F32)],
        compiler_params=_cparams(2),
        name="conv_module",
    )(u, u, prefix, conv_w, conv_b, ln_g, ln_b)


def _pair_scores(q2, kb):
    lane = lax.broadcasted_iota(jnp.int32, (CHUNK, PAIR), 1)
    lo = lane < HEAD_DIM
    zero = jnp.zeros_like(q2)
    qs = jnp.concatenate([jnp.where(lo, q2, zero), jnp.where(lo, zero, q2)], axis=0)
    return lax.dot_general(qs, kb, (((1,), (1,)), ((), ())), preferred_element_type=F32)


def _pair_softmax_pv(s, bias, vb, n_invalid):
    s = s + bias
    if n_invalid is None:
        first = s[:, :LANES]
        col = lax.broadcasted_iota(jnp.int32, first.shape, 1)
        s = jnp.concatenate([jnp.where(col >= CHUNK, first, NEG), s[:, LANES:]], axis=1)
    else:
        col = lax.broadcasted_iota(jnp.int32, s.shape, 1)
        s = jnp.where(col >= n_invalid, s, NEG)
    m = jnp.max(s, axis=-1, keepdims=True)
    e = jnp.exp2(s - m)
    l = jnp.sum(e, axis=-1, keepdims=True)
    r = jnp.dot(e.astype(BF16), vb, preferred_element_type=F32)
    r = r * (1.0 / l)
    lane = lax.broadcasted_iota(jnp.int32, (CHUNK, PAIR), 1)
    return jnp.where(lane < HEAD_DIM, r[:CHUNK], r[CHUNK:])


def _attn_prompt_kernel(q_ref, k_ref, v_ref, bias_ref, o_ref, kp_ref, vp_ref, s_ref, *, seq_len, npairs):
    pad = BAND_PAD - CHUNK
    kp_ref[0:pad, :] = jnp.zeros((pad, kp_ref.shape[1]), BF16)
    vp_ref[0:pad, :] = jnp.zeros((pad, vp_ref.shape[1]), BF16)
    kp_ref[pad:, :] = k_ref[...]
    vp_ref[pad:, :] = v_ref[...]
    n_chunks = seq_len // CHUNK
    n_masked = min(PAST_CHUNKS, n_chunks)
    assert n_chunks % 2 == 0 and n_masked % 2 == 0

    def scores(c, slot):
        r0 = pl.multiple_of(c * CHUNK, CHUNK)
        for hp in range(npairs):
            cols = slice(hp * PAIR, (hp + 1) * PAIR)
            s_ref[slot, hp] = _pair_scores(q_ref[pl.ds(r0, CHUNK), cols], kp_ref[pl.ds(r0, BAND_PAD), cols])

    def finish(c, slot, masked):
        r0 = pl.multiple_of(c * CHUNK, CHUNK)
        n_invalid = (PAST_CHUNKS + 1 - c) * CHUNK if masked else None
        for hp in range(npairs):
            cols = slice(hp * PAIR, (hp + 1) * PAIR)
            o2 = _pair_softmax_pv(s_ref[slot, hp], bias_ref[hp], vp_ref[pl.ds(r0, BAND_PAD), cols], n_invalid)
            o_ref[pl.ds(r0, CHUNK), cols] = o2.astype(o_ref.dtype)

    def two_chunks(cc, masked):
        c = 2 * cc
        scores(c + 1, 1)
        finish(c, 0, masked)
        scores(jnp.minimum(c + 2, n_chunks - 1), 0)
        finish(c + 1, 1, masked)

    def masked_body(cc, carry):
        two_chunks(cc, True)
        return carry

    def plain_body(cc, carry):
        two_chunks(cc, False)
        return carry

    scores(0, 0)
    lax.fori_loop(0, n_masked // 2, masked_body, 0)
    lax.fori_loop(n_masked // 2, n_chunks // 2, plain_body, 0)


def _attn_prompt(qkv, bias, layer, *, seq_len, group_pairs=2):
    n, da = qkv.shape[0], qkv.shape[1] // 3
    b = n // seq_len
    gp = min(group_pairs, da // PAIR)
    gw = gp * PAIR
    assert da % gw == 0
    ng = da // gw
    kern = functools.partial(_attn_prompt_kernel, seq_len=seq_len, npairs=gp)
    part = lambda k: pl.BlockSpec((seq_len, gw), lambda i, g: (i, k * ng + g))
    return pl.pallas_call(
        kern,
        grid=(b, ng),
        in_specs=[part(0), part(1), part(2),
                  pl.BlockSpec((None, gp, PAIR, BAND_PAD), lambda i, g: (layer, g, 0, 0))],
        out_specs=part(0),
        out_shape=jax.ShapeDtypeStruct((n, da), BF16),
        scratch_shapes=[pltpu.VMEM((seq_len + BAND_PAD - CHUNK, gw), BF16),
                        pltpu.VMEM((seq_len + BAND_PAD - CHUNK, gw), BF16),
                        pltpu.VMEM((2, gp, PAIR, BAND_PAD), F32)],
        compiler_params=_cparams(2),
        name="attn_prompt",
    )(qkv, qkv, qkv, bias)


def _attn_sample_kernel(q_ref, k_ref, v_ref, kt_ref, vt_ref, ck_ref, cv_ref, bias_ref,
                        o_ref, nk_ref, nv_ref, kb_ref, vb_ref, *, npairs):
    keep = ck_ref.shape[0]
    da = ck_ref.shape[1]
    kb_ref[0:CHUNK, :] = jnp.zeros((CHUNK, da), BF16)
    vb_ref[0:CHUNK, :] = jnp.zeros((CHUNK, da), BF16)
    kb_ref[CHUNK:CHUNK + keep, :] = ck_ref[...].astype(BF16)
    vb_ref[CHUNK:CHUNK + keep, :] = cv_ref[...].astype(BF16)
    kb_ref[CHUNK + keep:, :] = k_ref[...]
    vb_ref[CHUNK + keep:, :] = v_ref[...]
    nk_ref[0:keep - CHUNK, :] = ck_ref[CHUNK:, :]
    nv_ref[0:keep - CHUNK, :] = cv_ref[CHUNK:, :]
    nk_ref[keep - CHUNK:, :] = kt_ref[...]
    nv_ref[keep - CHUNK:, :] = vt_ref[...]
    cols = [slice(hp * PAIR, (hp + 1) * PAIR) for hp in range(npairs)]
    ss = [_pair_scores(q_ref[:, c], kb_ref[:, c]) for c in cols]
    for hp, c in enumerate(cols):
        o_ref[:, c] = _pair_softmax_pv(ss[hp], bias_ref[hp], vb_ref[:, c], None).astype(o_ref.dtype)


def _attn_sample(qkv, kt, vt, cache_k, cache_v, bias, layer):
    n, da = qkv.shape[0], qkv.shape[1] // 3
    b = n // CHUNK
    keep = cache_k.shape[2]
    assert keep == BAND_PAST
    npairs = da // PAIR
    kern = functools.partial(_attn_sample_kernel, npairs=npairs)
    part = lambda k: pl.BlockSpec((CHUNK, da), lambda i: (i, k))
    cache = pl.BlockSpec((None, None, keep, da), lambda i: (layer, i, 0, 0))
    newc = pl.BlockSpec((None, keep, da), lambda i: (i, 0, 0))
    return pl.pallas_call(
        kern,
        grid=(b,),
        in_specs=[part(0), part(1), part(2), part(0), part(0), cache, cache,
                  pl.BlockSpec((None, npairs, PAIR, BAND_PAD), lambda i: (layer, 0, 0, 0))],
        out_specs=[part(0), newc, newc],
        out_shape=[jax.ShapeDtypeStruct((n, da), BF16),
                   jax.ShapeDtypeStruct((b, keep, da), F32),
                   jax.ShapeDtypeStruct((b, keep, da), F32)],
        scratch_shapes=[pltpu.VMEM((BAND_PAD, da), BF16), pltpu.VMEM((BAND_PAD, da), BF16)],
        compiler_params=_cparams(1),
        name="attn_sample",
    )(qkv, qkv, qkv, kt, vt, cache_k, cache_v, bias)


def _band_bias(rel_bias):
    depth, h, _ = rel_bias.shape
    n_far = BAND + CHUNK - 1 - REL_CLIP + 1
    n_near = BAND_PAD + CHUNK - 1 - n_far
    far = jnp.broadcast_to(rel_bias[:, :, 2 * REL_CLIP:], (depth, h, n_far))
    near = lax.rev(rel_bias[:, :, 2 * REL_CLIP - n_near:2 * REL_CLIP], (2,))
    g = jnp.concatenate([far, near], axis=2)
    g = g.astype(F32) * LOG2E
    rows = [g[:, :, CHUNK - 1 - i:CHUNK - 1 - i + BAND_PAD] for i in range(CHUNK)]
    return jnp.stack(rows, axis=2).reshape(depth, h // 2, PAIR, BAND_PAD)


def _gatemix_kernel(h_ref, c_ref, a_ref, wco_ref, wo_ref, wg1_ref, wg2_ref, b1_ref, b2_ref, m_ref):
    h = h_ref[...]
    gc = _sigmoid(jnp.dot(h, wg1_ref[...], preferred_element_type=F32) + b1_ref[...])
    yc = jnp.dot(c_ref[...], wco_ref[...], preferred_element_type=F32)
    mc = gc * yc
    ga = _sigmoid(jnp.dot(h, wg2_ref[...], preferred_element_type=F32) + b2_ref[...])
    ya = jnp.dot(a_ref[...], wo_ref[...], preferred_element_type=F32)
    m_ref[...] = (mc + ga * ya).astype(BF16)


def _gatemix(h, cb, ab, wco, wo, wg, bg, layer, *, tm_pref=1024, tn=512):
    n, d = h.shape
    dc, da = cb.shape[1], ab.shape[1]
    tm = _row_tile(n, tm_pref)
    assert d % tn == 0
    nc = d // tn
    return pl.pallas_call(
        _gatemix_kernel,
        grid=(n // tm, nc),
        in_specs=[
            pl.BlockSpec((tm, d), lambda i, p: (i, 0)),
            pl.BlockSpec((tm, dc), lambda i, p: (i, 0)),
            pl.BlockSpec((tm, da), lambda i, p: (i, 0)),
            pl.BlockSpec((None, dc, tn), lambda i, p: (layer, 0, p)),
            pl.BlockSpec((None, da, tn), lambda i, p: (layer, 0, p)),
            pl.BlockSpec((None, d, tn), lambda i, p: (layer, 0, p)),
            pl.BlockSpec((None, d, tn), lambda i, p: (layer, 0, nc + p)),
            pl.BlockSpec((None, 1, tn), lambda i, p: (layer, 0, p)),
            pl.BlockSpec((None, 1, tn), lambda i, p: (layer, 0, nc + p)),
        ],
        out_specs=pl.BlockSpec((tm, tn), lambda i, p: (i, p)),
        out_shape=jax.ShapeDtypeStruct((n, d), BF16),
        compiler_params=_cparams(2),
        name="gatemix",
    )(h, cb, ab, wco, wo, wg, wg, bg, bg)


def _ffn_kernel(x_ref, m_ref, wout_ref, g_ref, wg_ref, wu_ref, wdn_ref, gp_ref, o_ref, *rest, nf, final_norm):
    hn_ref, h_ref = (None, rest[0]) if final_norm else rest
    j = pl.program_id(1)

    @pl.when(j == 0)
    def _():
        x1 = x_ref[...] + jnp.dot(m_ref[...], wout_ref[...], preferred_element_type=F32)
        h_ref[...] = _rms_rows(x1, g_ref[...]).astype(BF16)
        o_ref[...] = x1

    h = h_ref[...]
    gate = jnp.dot(h, wg_ref[...], preferred_element_type=F32)
    up = jnp.dot(h, wu_ref[...], preferred_element_type=F32)
    act = ((gate * _sigmoid(gate)) * up).astype(BF16)
    o_ref[...] += jnp.dot(act, wdn_ref[...], preferred_element_type=F32)

    @pl.when(j == nf - 1)
    def _():
        y = _rms_rows(o_ref[...], gp_ref[...])
        if final_norm:
            o_ref[...] = y
        else:
            hn_ref[...] = y.astype(BF16)


def _ffn(x, m, wout, g, wup, wdn, g_post, post_idx, layer, *, final_norm, tm_pref=512, tf=512):
    n, d = x.shape
    dff = wdn.shape[1]
    tm = _row_tile(n, tm_pref)
    assert dff % tf == 0
    nf = dff // tf
    kern = functools.partial(_ffn_kernel, nf=nf, final_norm=final_norm)
    rows = pl.BlockSpec((tm, d), lambda i, j: (i, 0))
    out_specs, out_shape = [rows], [jax.ShapeDtypeStruct((n, d), F32)]
    if not final_norm:
        out_specs.append(rows)
        out_shape.append(jax.ShapeDtypeStruct((n, d), BF16))
    return pl.pallas_call(
        kern,
        grid=(n // tm, nf),
        in_specs=[
            rows,
            rows,
            pl.BlockSpec((None, d, d), lambda i, j: (layer, 0, 0), pipeline_mode=pl.Buffered(1)),
            pl.BlockSpec((None, 1, d), lambda i, j: (layer, 0, 0)),
            pl.BlockSpec((None, d, tf), lambda i, j: (layer, 0, j)),
            pl.BlockSpec((None, d, tf), lambda i, j: (layer, 0, nf + j)),
            pl.BlockSpec((None, tf, d), lambda i, j: (layer, j, 0)),
            pl.BlockSpec((None, 1, d), lambda i, j: (post_idx, 0, 0)),
        ],
        out_specs=out_specs,
        out_shape=out_shape,
        scratch_shapes=[pltpu.VMEM((tm, d), BF16)],
        compiler_params=_cparams(2),
        name="ffn",
    )(x, m, wout, g, wup, wup, wdn, g_post)


def kernel(x_prompt, x_sample, cache_k, cache_v, state_conv, norm_mix, w_in, conv_w, conv_b, ln_g, ln_b,
           w_conv_out, rel_bias, w_o, w_gate, b_gate, w_out, norm_ffn, w_up, w_down, norm_final):
    depth = w_in.shape[0]
    bp, tp, d = x_prompt.shape
    bs, ts, _ = x_sample.shape
    dc = conv_w.shape[2]
    da = w_o.shape[1]
    dff = w_down.shape[1]
    keep_s = cache_k.shape[2]
    nh = da // HEAD_DIM
    n_state = CONV_WIDTH - 1
    assert ts == CHUNK and tp % CHUNK == 0
    tf = 512 if dff % 512 == 0 else 256

    w_ab_b, w_qkv_b = w_in[:, :, :2 * dc].astype(BF16), w_in[:, :, 2 * dc:].astype(BF16)
    w_up_b, w_down_b, w_co_b, w_o_b, w_gate_b, w_out_b = (
        w.astype(BF16) for w in (w_up, w_down, w_conv_out, w_o, w_gate, w_out))
    row3 = lambda a: a.reshape(depth, 1, a.shape[-1])
    g_mix, g_ffn, cb3, lg3, lb3, bg3 = map(row3, (norm_mix, norm_ffn, conv_b, ln_g, ln_b, b_gate))
    g_fin = norm_final.reshape(1, 1, d)
    bias = _band_bias(rel_bias)
    ck = cache_k.reshape(depth, bs, keep_s, da)
    cv = cache_v.reshape(depth, bs, keep_s, da)
    conv_p = (jnp.broadcast_to(conv_w[:, :, None, :], (depth, CONV_WIDTH, SUBLANES, dc)),
              jnp.broadcast_to(cb3, (depth, SUBLANES, dc)), lg3, lb3)

    xp = x_prompt.reshape(bp * tp, d)
    xs = x_sample.reshape(bs * ts, d)
    keep_p = min(BAND_PAST, tp)
    heads = lambda a, b, keep: a.reshape(b, keep, nh, HEAD_DIM)
    kp, vp, cp, ksl, vsl, csl = [], [], [], [], [], []
    qkv_scale = jnp.where(jnp.arange(3 * da) < da, QK_SCALE, 1.0).astype(F32).reshape(1, 3 * da)
    hp, hs = _first_norm(xp, g_mix), _first_norm(xs, g_mix)
    for l in range(depth):
        last = l == depth - 1
        in_w = (w_ab_b, w_qkv_b, qkv_scale, l)
        mix_w = (w_co_b, w_o_b, w_gate_b, bg3, l)
        ffn_w = (w_out_b, g_ffn, w_up_b, w_down_b) + ((g_fin, 0, l) if last else (g_mix, l + 1, l))
        cact, qkv, kt, vt, ust = _inproj(hp, *in_w, seq_len=tp, conv=conv_p, tm_pref=1024)
        att = _attn_prompt(qkv, bias, l, seq_len=tp)
        out = _ffn(xp, _gatemix(hp, cact, att, *mix_w), *ffn_w, final_norm=last, tf=tf)
        xp, hp = (out[0], None) if last else out
        kp.append(heads(kt, bp, keep_p))
        vp.append(heads(vt, bp, keep_p))
        cp.append(ust[:, CONV_HALO - n_state:])
        u, qkv, kt, vt = _inproj(hs, *in_w, seq_len=ts)
        att, nk, nv = _attn_sample(qkv, kt, vt, ck, cv, bias, l)
        pre_s = jnp.pad(state_conv[l], ((0, 0), (CONV_HALO - n_state, 0), (0, 0)))
        cact = _conv_module(u, pre_s, *conv_p, l, seq_len=ts)
        out = _ffn(xs, _gatemix(hs, cact, att, *mix_w), *ffn_w, final_norm=last, tf=tf)
        xs, hs = (out[0], None) if last else out
        ksl.append(heads(nk, bs, keep_s))
        vsl.append(heads(nv, bs, keep_s))
        csl.append(u.reshape(bs, ts, dc)[:, ts - n_state:])
    return (xp.reshape(bp, tp, d), xs.reshape(bs, ts, d), jnp.stack(kp), jnp.stack(vp), jnp.stack(cp),
            jnp.stack(ksl), jnp.stack(vsl), jnp.stack(csl))
```
